```python
import math
import jax, jax.numpy as jnp
from jax import lax
import numpy as np

D_MODEL = 1024
BATCH = 4
SEQ = 4096
DEPTH = 4

CHUNK = 64
Q_BLOCK = 128
N_MIXERS = 3

MLA_HEADS = 16
QK_NOPE = 64
QK_ROPE = 32
V_HEAD = 64
Q_LORA = 384
KV_LORA = 256
ROPE_THETA = 10000.0

CONV_WIDTH = 31

POOL_WINDOWS = (2, 4, 8, 16)
POOL_GROUPS = len(POOL_WINDOWS)

D_FF = 4 * D_MODEL

NORM_EPS = 1e-6
NEG_INF = -1e30

kernel_name = "hybrid_mla_conformer_pool_trunk"


def _rmsnorm(x, g):
    x32 = x.astype(jnp.float32)
    y = x32 * lax.rsqrt(jnp.mean(x32 * x32, axis=-1, keepdims=True) + NORM_EPS)
    return (y * g.astype(jnp.float32)).astype(x.dtype)


def _layernorm(x, g, b):
    x32 = x.astype(jnp.float32)
    mu = jnp.mean(x32, axis=-1, keepdims=True)
    xc = x32 - mu
    y = xc * lax.rsqrt(jnp.mean(xc * xc, axis=-1, keepdims=True) + NORM_EPS)
    return (y * g.astype(jnp.float32) + b.astype(jnp.float32)).astype(x.dtype)


def _rope_tables(positions):
    inv_freq = ROPE_THETA ** (-jnp.arange(0, QK_ROPE, 2, dtype=jnp.float32) / QK_ROPE)
    ang = positions.astype(jnp.float32)[..., None] * inv_freq
    return jnp.cos(ang), jnp.sin(ang)


def _apply_rope(x, cos, sin):
    half = x.shape[-1] // 2
    x1 = x[..., :half].astype(jnp.float32)
    x2 = x[..., half:].astype(jnp.float32)
    out = jnp.concatenate([x1 * cos - x2 * sin, x1 * sin + x2 * cos], axis=-1)
    return out.astype(x.dtype)


def _chunk_causal_attention(q, k, v):
    B, S, H, Dk = q.shape
    nb = S // Q_BLOCK
    qb = q.reshape(B, nb, Q_BLOCK, H, Dk).swapaxes(0, 1)
    k_chunk = jnp.arange(S) // CHUNK
    scale = 1.0 / math.sqrt(Dk)

    def one_block(args):
        qblk, bi = args
        q_chunk = (bi * Q_BLOCK + jnp.arange(Q_BLOCK)) // CHUNK
        mask = k_chunk[None, :] <= q_chunk[:, None]
        s = jnp.einsum('bqhd,bkhd->bhqk', qblk, k).astype(jnp.float32) * scale
        s = jnp.where(mask[None, None], s, NEG_INF)
        p = jax.nn.softmax(s, axis=-1).astype(v.dtype)
        return jnp.einsum('bhqk,bkhd->bqhd', p, v)

    o = lax.map(one_block, (qb, jnp.arange(nb)))
    return o.swapaxes(0, 1).reshape(B, S, H, v.shape[-1])


def _mla(h, positions, w_dq, q_norm_g, w_uq, w_dkv, kv_norm_g, w_ukv, w_o):
    B, S, _ = h.shape
    cq = _rmsnorm(h @ w_dq, q_norm_g)
    q = (cq @ w_uq).reshape(B, S, MLA_HEADS, QK_NOPE + QK_ROPE)
    q_nope, q_rope = q[..., :QK_NOPE], q[..., QK_NOPE:]
    ckv_all = h @ w_dkv
    ckv = _rmsnorm(ckv_all[..., :KV_LORA], kv_norm_g)
    k_rope = ckv_all[..., KV_LORA:]
    kv = (ckv @ w_ukv).reshape(B, S, MLA_HEADS, QK_NOPE + V_HEAD)
    k_nope, v = kv[..., :QK_NOPE], kv[..., QK_NOPE:]
    cos, sin = _rope_tables(positions)
    q_rope = _apply_rope(q_rope, cos[:, :, None], sin[:, :, None])
    k_rope = _apply_rope(k_rope, cos, sin)
    qf = jnp.concatenate([q_nope, q_rope], axis=-1)
    kf = jnp.concatenate(
        [k_nope, jnp.broadcast_to(k_rope[:, :, None], (B, S, MLA_HEADS, QK_ROPE))], axis=-1)
    o = _chunk_causal_attention(qf, kf, v)
    return o.reshape(B, S, MLA_HEADS * V_HEAD) @ w_o


def _conformer_conv(h, w_pw1, b_pw1, w_dw, b_dw, ln_g, ln_b, w_pw2, b_pw2):
    D = h.shape[-1]
    a = h @ w_pw1 + b_pw1
    u = a[..., :D] * jax.nn.sigmoid(a[..., D:])
    u = lax.conv_general_dilated(
        u, w_dw[:, None, :].astype(u.dtype), window_strides=(1,),
        padding=[(CONV_WIDTH - 1, 0)], dimension_numbers=('NWC', 'WIO', 'NWC'),
        feature_group_count=D) + b_dw
    u = jax.nn.silu(_layernorm(u, ln_g, ln_b))
    return u @ w_pw2 + b_pw2


def _pool_mixer(h, w, b, scale):
    B, S, D = h.shape
    C = D // POOL_GROUPS
    csum = jnp.cumsum(h.astype(jnp.float32), axis=1)
    t = jnp.arange(S)
    pooled = []
    for g, win in enumerate(POOL_WINDOWS):
        cs = csum[..., g * C:(g + 1) * C]
        lag = jnp.pad(cs, ((0, 0), (win, 0), (0, 0)))[:, :S]
        cnt = jnp.minimum(t + 1, win).astype(jnp.float32)
        pooled.append((cs - lag) / cnt[None, :, None])
    p = jnp.concatenate(pooled, axis=-1).astype(h.dtype) - h
    y = jnp.einsum('bsgc,gcd->bsgd', p.reshape(B, S, POOL_GROUPS, C), w) + b
    return y.reshape(B, S, D) * scale


def _sq_relu_mlp(h, w1, w2):
    return jnp.square(jax.nn.relu(h @ w1)) @ w2


def setup_inputs(seed: int = 0) -> dict:
    key = jax.random.key(seed)
    ks = jax.random.split(key, 32)
    D = D_MODEL
    n_mla = (DEPTH + 2) // 3
    n_conv = (DEPTH + 1) // 3
    n_pool = DEPTH // 3
    C = D // POOL_GROUPS

    def nrm(k, shape, fan_in, mult=1.0):
        return jax.random.normal(k, shape, jnp.float32) * (mult * fan_in ** -0.5)

    def gain(k, shape):
        return 1.0 + 0.05 * jax.random.normal(k, shape, jnp.float32)

    def bias(k, shape):
        return 0.02 * jax.random.normal(k, shape, jnp.float32)

    x = jax.random.normal(ks[0], (BATCH, SEQ, D), jnp.float32)
    c = jax.random.normal(ks[1], (BATCH, D), jnp.float32)
    offsets = jax.random.randint(ks[2], (BATCH,), 0, 64, dtype=jnp.int32) * CHUNK
    positions = offsets[:, None] + jnp.arange(SEQ, dtype=jnp.int32)[None, :]
    return {
        "x": x,
        "c": c,
        "positions": positions,
        "ada_w": nrm(ks[3], (DEPTH, D, 6 * D), D, 0.5),
        "ada_b": bias(ks[4], (DEPTH, 6 * D)),
        "norm_g": gain(ks[5], (DEPTH, 4, D)),
        "mla_w_dq": nrm(ks[6], (n_mla, D, Q_LORA), D),
        "mla_q_norm_g": gain(ks[7], (n_mla, Q_LORA)),
        "mla_w_uq": nrm(ks[8], (n_mla, Q_LORA, MLA_HEADS * (QK_NOPE + QK_ROPE)), Q_LORA),
        "mla_w_dkv": nrm(ks[9], (n_mla, D, KV_LORA + QK_ROPE), D),
        "mla_kv_norm_g": gain(ks[10], (n_mla, KV_LORA)),
        "mla_w_ukv": nrm(ks[11], (n_mla, KV_LORA, MLA_HEADS * (QK_NOPE + V_HEAD)), KV_LORA),
        "mla_w_o": nrm(ks[12], (n_mla, MLA_HEADS * V_HEAD, D), MLA_HEADS * V_HEAD),
        "conv_w_pw1": nrm(ks[13], (n_conv, D, 2 * D), D),
        "conv_b_pw1": bias(ks[14], (n_conv, 2 * D)),
        "conv_w_dw": nrm(ks[15], (n_conv, CONV_WIDTH, D), CONV_WIDTH),
        "conv_b_dw": bias(ks[16], (n_conv, D)),
        "conv_ln_g": gain(ks[17], (n_conv, D)),
        "conv_ln_b": bias(ks[18], (n_conv, D)),
        "conv_w_pw2": nrm(ks[19], (n_conv, D, D), D),
        "conv_b_pw2": bias(ks[20], (n_conv, D)),
        "pool_w": nrm(ks[21], (n_pool, POOL_GROUPS, C, C), C),
        "pool_b": bias(ks[22], (n_pool, POOL_GROUPS, C)),
        "pool_scale": gain(ks[23], (n_pool, D)),
        "ffn_w1": nrm(ks[24], (DEPTH, D, D_FF), D),
        "ffn_w2": nrm(ks[25], (DEPTH, D_FF, D), D_FF),
    }


def reference(x, c, positions, ada_w, ada_b, norm_g,
              mla_w_dq, mla_q_norm_g, mla_w_uq, mla_w_dkv, mla_kv_norm_g, mla_w_ukv, mla_w_o,
              conv_w_pw1, conv_b_pw1, conv_w_dw, conv_b_dw, conv_ln_g, conv_ln_b,
              conv_w_pw2, conv_b_pw2,
              pool_w, pool_b, pool_scale,
              ffn_w1, ffn_w2):
    c_act = jax.nn.silu(c)
    for i in range(DEPTH):
        kind = i % N_MIXERS
        j = i // N_MIXERS
        mod = c_act @ ada_w[i] + ada_b[i]
        sh_m, sc_m, gt_m, sh_f, sc_f, gt_f = jnp.split(mod, 6, axis=-1)

        h = _rmsnorm(x, norm_g[i, 0]) * (1.0 + sc_m[:, None]) + sh_m[:, None]
        if kind == 0:
            y = _mla(h, positions, mla_w_dq[j], mla_q_norm_g[j], mla_w_uq[j],
                     mla_w_dkv[j], mla_kv_norm_g[j], mla_w_ukv[j], mla_w_o[j])
        elif kind == 1:
            y = _conformer_conv(h, conv_w_pw1[j], conv_b_pw1[j], conv_w_dw[j], conv_b_dw[j],
                                conv_ln_g[j], conv_ln_b[j], conv_w_pw2[j], conv_b_pw2[j])
        else:
            y = _pool_mixer(h, pool_w[j], pool_b[j], pool_scale[j])
        x = x + gt_m[:, None] * _rmsnorm(y, norm_g[i, 1])

        h = _rmsnorm(x, norm_g[i, 2]) * (1.0 + sc_f[:, None]) + sh_f[:, None]
        y = _sq_relu_mlp(h, ffn_w1[i], ffn_w2[i])
        x = x + gt_f[:, None] * _rmsnorm(y, norm_g[i, 3])
    return x
```

```python
import functools
import math

import jax
import jax.numpy as jnp
from jax import lax
from jax.experimental import pallas as pl
from jax.experimental.pallas import tpu as pltpu

D_MODEL = 1024
DEPTH = 4
CHUNK = 64
N_MIXERS = 3
MLA_HEADS = 16
QK_NOPE = 64
QK_ROPE = 32
V_HEAD = 64
Q_LORA = 384
KV_LORA = 256
ROPE_THETA = 10000.0
CONV_WIDTH = 31
POOL_WINDOWS = (2, 4, 8, 16)
D_FF = 4 * D_MODEL
NORM_EPS = 1e-6
NEG_INF = -1e30

LANES = 128
HEAD_PAD = LANES
ROPE_LANE0 = QK_NOPE
ONES_LANE = V_HEAD
VMEM_LIMIT_BYTES = 56 * 1024 * 1024

TOKEN_TILE = 512
FF_CHUNK = 1024
ATTN_TILE = 512
CONV_ROWS = 32
HALO = 32
ADA_COLS = 1536

F32 = jnp.float32
BF16 = jnp.bfloat16


def _const_spec(shape):
    zeros = (0,) * len(shape)
    return pl.BlockSpec(shape, lambda *_: zeros, pipeline_mode=pl.Buffered(1))


def _params(*semantics):
    return pltpu.CompilerParams(dimension_semantics=semantics,
                                vmem_limit_bytes=VMEM_LIMIT_BYTES)


def _dot(a, b):
    return jnp.dot(a, b, preferred_element_type=F32)


def _rms(x, g):
    return x * lax.rsqrt(jnp.mean(x * x, axis=-1, keepdims=True) + NORM_EPS) * g


def _ffn_sublayer(x, mod, ng, w1_ref, w2_ref):
    h = (_rms(x, ng[2:3]) * (1.0 + mod[4:5]) + mod[3:4]).astype(BF16)
    y = jnp.zeros(x.shape, F32)
    for c in range(D_FF // FF_CHUNK):
        cols = slice(c * FF_CHUNK, (c + 1) * FF_CHUNK)
        a = jnp.maximum(_dot(h, w1_ref[:, cols]), 0.0)
        y = y + _dot((a * a).astype(BF16), w2_ref[cols, :])
    return x + mod[5:6] * _rms(y, ng[3:4])


def _ada_kernel(c_ref, w_ref, b_ref, o_ref):
    c = c_ref[...]
    c_act = (c * jax.nn.sigmoid(c)).astype(BF16)
    o_ref[...] = _dot(c_act, w_ref[...].astype(BF16)) + b_ref[...]


def _ada_mod(c, ada_w, ada_b):
    batch = c.shape[0]
    n_cols = ada_w.shape[-1]
    out = pl.pallas_call(
        _ada_kernel,
        grid=(DEPTH, n_cols // ADA_COLS),
        in_specs=[
            pl.BlockSpec((batch, D_MODEL), lambda l, n: (0, 0)),
            pl.BlockSpec((None, D_MODEL, ADA_COLS), lambda l, n: (l, 0, n)),
            pl.BlockSpec((None, 1, ADA_COLS), lambda l, n: (l, 0, n)),
        ],
        out_specs=pl.BlockSpec((None, batch, ADA_COLS), lambda l, n: (l, 0, n)),
        out_shape=jax.ShapeDtypeStruct((DEPTH, batch, n_cols), F32),
        compiler_params=_params("arbitrary", "arbitrary"),
        name="ada_mod",
    )(c, ada_w, ada_b.reshape(DEPTH, 1, n_cols))
    return out.reshape(DEPTH, batch, 6, D_MODEL)


def _mla_proj_kernel(x_ref, pos_ref, mod_ref, ng_ref, invf_ref, w_dq_ref, qg_ref, w_uq_ref,
                     w_uqs_ref, w_dkv_ref, kvg_ref, w_uk_ref, w_uv_ref,
                     q_ref, k_ref, v_ref):
    mod = mod_ref[...]
    ng = ng_ref[...]
    h = (_rms(x_ref[...], ng[0:1]) * (1.0 + mod[1:2]) + mod[0:1]).astype(BF16)

    ang = pos_ref[...].astype(F32) * invf_ref[...]
    lane = lax.broadcasted_iota(jnp.int32, (1, LANES), 1)
    first_half = (lane >= ROPE_LANE0) & (lane < ROPE_LANE0 + QK_ROPE // 2)
    second_half = (lane >= ROPE_LANE0 + QK_ROPE // 2) & (lane < ROPE_LANE0 + QK_ROPE)
    sin = jnp.sin(ang)
    cos_t = jnp.where(first_half | second_half, jnp.cos(ang), 1.0)
    sin_t = jnp.where(first_half, -sin, jnp.where(second_half, sin, 0.0))

    cq = _rms(_dot(h, w_dq_ref[...]), qg_ref[...]).astype(BF16)
    ckv_all = _dot(h, w_dkv_ref[...])
    ckv = _rms(ckv_all[:, :KV_LORA], kvg_ref[...]).astype(BF16)
    k_rope = (ckv_all[:, KV_LORA:KV_LORA + LANES] * cos_t
              + ckv_all[:, KV_LORA + LANES:] * sin_t)

    scale = 1.0 / math.sqrt(QK_NOPE + QK_ROPE)
    lane_v = lax.broadcasted_iota(jnp.int32, (1, LANES), 1)
    ones_col = jnp.where(lane_v == ONES_LANE, 1.0, 0.0)
    group = 4 * HEAD_PAD
    for g in range(MLA_HEADS * HEAD_PAD // group):
        cols = slice(g * group, (g + 1) * group)
        q = _dot(cq, w_uq_ref[:, cols])
        q_sw = _dot(cq, w_uqs_ref[:, cols])
        k = _dot(ckv, w_uk_ref[:, cols])
        v = _dot(ckv, w_uv_ref[:, cols])
        for j in range(group // HEAD_PAD):
            sl = slice(j * HEAD_PAD, (j + 1) * HEAD_PAD)
            out = slice(g * group + j * HEAD_PAD, g * group + (j + 1) * HEAD_PAD)
            q_ref[:, out] = ((q[:, sl] * cos_t + q_sw[:, sl] * sin_t) * scale).astype(BF16)
            k_ref[:, out] = (k[:, sl] + k_rope).astype(BF16)
            v_ref[:, out] = (v[:, sl] + ones_col).astype(BF16)


def _mla_proj(x, pos3, mod, ng, invf, w):
    batch, seq, _ = x.shape
    tm = TOKEN_TILE
    wide = MLA_HEADS * HEAD_PAD
    tile = lambda width: pl.BlockSpec((None, tm, width), lambda b, t: (b, t, 0))
    out_sds = jax.ShapeDtypeStruct((batch, seq, wide), BF16)
    return pl.pallas_call(
        _mla_proj_kernel,
        grid=(batch, seq // tm),
        in_specs=[
            tile(D_MODEL),
            tile(1),
            pl.BlockSpec((None, 6, D_MODEL), lambda b, t: (b, 0, 0)),
            _const_spec((4, D_MODEL)),
            _const_spec((1, LANES)),
            _const_spec(w["w_dq"].shape),
            _const_spec(w["qg"].shape),
            _const_spec(w["w_uq"].shape),
            _const_spec(w["w_uqs"].shape),
            _const_spec(w["w_dkv"].shape),
            _const_spec(w["kvg"].shape),
            _const_spec(w["w_uk"].shape),
            _const_spec(w["w_uv"].shape),
        ],
        out_specs=[tile(wide), tile(wide), tile(wide)],
        out_shape=[out_sds, out_sds, out_sds],
        compiler_params=_params("arbitrary", "arbitrary"),
        name="mla_proj",
    )(x, pos3, mod, ng, invf, w["w_dq"], w["qg"], w["w_uq"], w["w_uqs"], w["w_dkv"], w["kvg"],
      w["w_uk"], w["w_uv"])


def _attn_kernel(q_ref, k_ref, v_ref, o_ref, m_sc, acc_sc):
    t = ATTN_TILE
    i = pl.program_id(2)
    q = q_ref[...]
    m_sc[...] = jnp.full(m_sc.shape, NEG_INF, F32)
    acc_sc[...] = jnp.zeros(acc_sc.shape, F32)

    def block(j, masked):
        rows = pl.ds(pl.multiple_of(j * t, t), t)
        s = lax.dot_general(q, k_ref[rows, :], (((1,), (1,)), ((), ())),
                            preferred_element_type=F32)
        if masked:
            q_chunk = lax.broadcasted_iota(jnp.int32, (t, t), 0) // CHUNK
            k_chunk = lax.broadcasted_iota(jnp.int32, (t, t), 1) // CHUNK
            s = jnp.where(k_chunk <= q_chunk, s, NEG_INF)
        m_old = m_sc[...]
        m_new = jnp.maximum(m_old, jnp.max(s, axis=-1, keepdims=True))
        p = jnp.exp(s - m_new).astype(BF16)
        acc_sc[...] = jnp.exp(m_old - m_new) * acc_sc[...] + _dot(p, v_ref[rows, :])
        m_sc[...] = m_new

    def body(j, carry):
        block(j, masked=False)
        return carry

    lax.fori_loop(0, i, body, 0)
    block(i, masked=True)
    acc = acc_sc[...]
    o_ref[...] = (acc / acc[:, ONES_LANE:ONES_LANE + 1]).astype(BF16)


def _attention(q, k, v):
    batch, seq, _ = q.shape
    t = ATTN_TILE
    return pl.pallas_call(
        _attn_kernel,
        grid=(batch, MLA_HEADS, seq // t),
        in_specs=[
            pl.BlockSpec((None, t, HEAD_PAD), lambda b, h, i: (b, i, h)),
            pl.BlockSpec((None, seq, HEAD_PAD), lambda b, h, i: (b, 0, h)),
            pl.BlockSpec((None, seq, HEAD_PAD), lambda b, h, i: (b, 0, h)),
        ],
        out_specs=pl.BlockSpec((None, t, HEAD_PAD), lambda b, h, i: (b, i, h)),
        out_shape=jax.ShapeDtypeStruct(q.shape, BF16),
        scratch_shapes=[pltpu.VMEM((t, 1), F32), pltpu.VMEM((t, HEAD_PAD), F32)],
        compiler_params=_params("arbitrary", "arbitrary", "arbitrary"),
        name="mla_attention",
    )(q, k, v)


def _mla_out_kernel(x_ref, o_ref, mod_ref, ng_ref, w_o_ref, w1_ref, w2_ref, out_ref):
    mod = mod_ref[...]
    ng = ng_ref[...]
    y = _dot(o_ref[...], w_o_ref[...])
    x = x_ref[...] + mod[2:3] * _rms(y, ng[1:2])
    out_ref[...] = _ffn_sublayer(x, mod, ng, w1_ref, w2_ref)


def _mla_out(x, o, mod, ng, w_o, w1, w2):
    batch, seq, _ = x.shape
    tm = TOKEN_TILE
    tile = lambda width: pl.BlockSpec((None, tm, width), lambda b, t: (b, t, 0))
    return pl.pallas_call(
        _mla_out_kernel,
        grid=(batch, seq // tm),
        in_specs=[
            tile(D_MODEL),
            tile(o.shape[-1]),
            pl.BlockSpec((None, 6, D_MODEL), lambda b, t: (b, 0, 0)),
            _const_spec((4, D_MODEL)),
            _const_spec(w_o.shape),
            _const_spec(w1.shape),
            _const_spec(w2.shape),
        ],
        out_specs=tile(D_MODEL),
        out_shape=jax.ShapeDtypeStruct(x.shape, F32),
        compiler_params=_params("arbitrary", "arbitrary"),
        name="mla_out_ffn",
    )(x, o, mod, ng, w_o, w1, w2)


def _conv_kernel(x_ref, mod_ref, ng_ref, w_pw1_ref, b_pw1_ref, w_dw_ref, b_dw_ref, ln_g_ref,
                 ln_b_ref, w_pw2_ref, b_pw2_ref, w1_ref, w2_ref, out_ref, u_sc, c_sc):
    tm = TOKEN_TILE
    mod = mod_ref[...]
    ng = ng_ref[...]
    x = x_ref[...]
    h = (_rms(x, ng[0:1]) * (1.0 + mod[1:2]) + mod[0:1]).astype(BF16)
    a = _dot(h, w_pw1_ref[...]) + b_pw1_ref[...]
    u = a[:, :D_MODEL] * jax.nn.sigmoid(a[:, D_MODEL:])

    @pl.when(pl.program_id(1) == 0)
    def _():
        u_sc[0:HALO, :] = jnp.zeros((HALO, D_MODEL), F32)

    @pl.when(pl.program_id(1) != 0)
    def _():
        u_sc[0:HALO, :] = u_sc[tm:tm + HALO, :]

    u_sc[HALO:HALO + tm, :] = u

    first_tap = HALO - (CONV_WIDTH - 1)
    w_dw = w_dw_ref[...]
    b_dw = b_dw_ref[...]

    for r0 in range(0, tm, CONV_ROWS):
        acc = jnp.broadcast_to(b_dw, (CONV_ROWS, D_MODEL))
        for j in range(CONV_WIDTH):
            acc = acc + u_sc[r0 + first_tap + j:r0 + first_tap + j + CONV_ROWS, :] * w_dw[j:j + 1, :]
        c_sc[r0:r0 + CONV_ROWS, :] = acc

    cv = c_sc[...]
    mu = jnp.mean(cv, axis=-1, keepdims=True)
    xc = cv - mu
    ln = xc * lax.rsqrt(jnp.mean(xc * xc, axis=-1, keepdims=True) + NORM_EPS)
    ln = ln * ln_g_ref[...] + ln_b_ref[...]
    act = (ln * jax.nn.sigmoid(ln)).astype(BF16)
    y = _dot(act, w_pw2_ref[...]) + b_pw2_ref[...]
    x = x + mod[2:3] * _rms(y, ng[1:2])
    out_ref[...] = _ffn_sublayer(x, mod, ng, w1_ref, w2_ref)


def _conv_layer(x, mod, ng, w, w1, w2):
    batch, seq, _ = x.shape
    tm = TOKEN_TILE
    tile = pl.BlockSpec((None, tm, D_MODEL), lambda b, t: (b, t, 0))
    names = ("w_pw1", "b_pw1", "w_dw", "b_dw", "ln_g", "ln_b", "w_pw2", "b_pw2")
    return pl.pallas_call(
        _conv_kernel,
        grid=(batch, seq // tm),
        in_specs=[tile,
                  pl.BlockSpec((None, 6, D_MODEL), lambda b, t: (b, 0, 0)),
                  _const_spec((4, D_MODEL))]
                 + [_const_spec(w[n].shape) for n in names]
                 + [_const_spec(w1.shape), _const_spec(w2.shape)],
        out_specs=tile,
        out_shape=jax.ShapeDtypeStruct(x.shape, F32),
        scratch_shapes=[pltpu.VMEM((HALO + tm, D_MODEL), F32), pltpu.VMEM((tm, D_MODEL), F32)],
        compiler_params=_params("arbitrary", "arbitrary"),
        name="conv_ffn",
    )(x, mod, ng, *[w[n] for n in names], w1, w2)


def _pool_kernel(x_ref, mod_ref, ng_ref, w_ref, b_ref, scale_ref, w1_ref, w2_ref, out_ref,
                 h_sc, s_a, s_b, y_sc):
    tm = TOKEN_TILE
    n_groups = len(POOL_WINDOWS)
    width = D_MODEL // n_groups
    mod = mod_ref[...]
    ng = ng_ref[...]
    x = x_ref[...]
    h = _rms(x, ng[0:1]) * (1.0 + mod[1:2]) + mod[0:1]

    @pl.when(pl.program_id(1) == 0)
    def _():
        h_sc[0:HALO, :] = jnp.zeros((HALO, D_MODEL), F32)

    @pl.when(pl.program_id(1) != 0)
    def _():
        h_sc[0:HALO, :] = h_sc[tm:tm + HALO, :]

    h_sc[HALO:HALO + tm, :] = h

    n = HALO + tm
    s_a[8:n, :] = h_sc[8:n, :] + h_sc[7:n - 1, :]
    s_b[16:n, width:] = s_a[16:n, width:] + s_a[14:n - 2, width:]
    s_a[24:n, 2 * width:] = s_b[24:n, 2 * width:] + s_b[20:n - 4, 2 * width:]
    s_b[32:n, 3 * width:] = s_a[32:n, 3 * width:] + s_a[24:n - 8, 3 * width:]

    t_idx = pl.program_id(1) * tm + lax.broadcasted_iota(jnp.int32, (tm, 1), 0)
    sums = (s_a, s_b, s_a, s_b)
    for g, win in enumerate(POOL_WINDOWS):
        cols = slice(g * width, (g + 1) * width)
        cnt = jnp.minimum(t_idx + 1, win).astype(F32)
        p = sums[g][HALO:n, cols] / cnt - h[:, cols]
        y_sc[:, cols] = _dot(p.astype(BF16), w_ref[g]) + b_ref[g:g + 1, :]

    y = y_sc[...] * scale_ref[...]
    x = x + mod[2:3] * _rms(y, ng[1:2])
    out_ref[...] = _ffn_sublayer(x, mod, ng, w1_ref, w2_ref)


def _pool_layer(x, mod, ng, w, b, scale, w1, w2):
    batch, seq, _ = x.shape
    tm = TOKEN_TILE
    tile = pl.BlockSpec((None, tm, D_MODEL), lambda b_, t: (b_, t, 0))
    halo_buf = pltpu.VMEM((HALO + tm, D_MODEL), F32)
    return pl.pallas_call(
        _pool_kernel,
        grid=(batch, seq // tm),
        in_specs=[tile,
                  pl.BlockSpec((None, 6, D_MODEL), lambda b_, t: (b_, 0, 0)),
                  _const_spec((4, D_MODEL)),
                  _const_spec(w.shape), _const_spec(b.shape), _const_spec(scale.shape),
                  _const_spec(w1.shape), _const_spec(w2.shape)],
        out_specs=tile,
        out_shape=jax.ShapeDtypeStruct(x.shape, F32),
        scratch_shapes=[halo_buf, halo_buf, halo_buf, pltpu.VMEM((tm, D_MODEL), F32)],
        compiler_params=_params("arbitrary", "arbitrary"),
        name="pool_ffn",
    )(x, mod, ng, w, b, scale, w1, w2)


def _pad_heads(w, per_head, lane0=0):
    k = w.shape[0]
    w = w.reshape(k, MLA_HEADS, per_head)
    w = jnp.pad(w, ((0, 0), (0, 0), (lane0, HEAD_PAD - lane0 - per_head)))
    return w.reshape(k, MLA_HEADS * HEAD_PAD)


def _swap_halves(w):
    half = w.shape[-1] // 2
    return jnp.concatenate([w[..., half:], w[..., :half]], axis=-1)


def _mla_weights(w_dq, q_norm_g, w_uq, w_dkv, kv_norm_g, w_ukv, w_o):
    uq = w_uq.reshape(Q_LORA, MLA_HEADS, QK_NOPE + QK_ROPE)
    uq_nope = uq[..., :QK_NOPE].reshape(Q_LORA, -1)
    uq_rope = uq[..., QK_NOPE:]
    rope_cols = lambda w3: _pad_heads(w3.reshape(Q_LORA, -1), QK_ROPE, ROPE_LANE0)
    w_uq_pad = _pad_heads(uq_nope, QK_NOPE) + rope_cols(uq_rope)
    w_uq_sw = rope_cols(_swap_halves(uq_rope))

    dkv_rope = w_dkv[:, KV_LORA:]
    pad_rope = lambda w: jnp.pad(w, ((0, 0), (ROPE_LANE0, LANES - ROPE_LANE0 - QK_ROPE)))
    w_dkv_all = jnp.concatenate(
        [w_dkv[:, :KV_LORA], pad_rope(dkv_rope), pad_rope(_swap_halves(dkv_rope))], axis=-1)

    ukv = w_ukv.reshape(KV_LORA, MLA_HEADS, QK_NOPE + V_HEAD)
    w_uk = _pad_heads(ukv[..., :QK_NOPE].reshape(KV_LORA, -1), QK_NOPE)
    w_uv = _pad_heads(ukv[..., QK_NOPE:].reshape(KV_LORA, -1), V_HEAD)

    o = w_o.reshape(MLA_HEADS, V_HEAD, D_MODEL)
    w_o_pad = jnp.pad(o, ((0, 0), (0, HEAD_PAD - V_HEAD), (0, 0))).reshape(-1, D_MODEL)
    return {
        "w_dq": w_dq.astype(BF16), "qg": q_norm_g.reshape(1, -1),
        "w_uq": w_uq_pad.astype(BF16), "w_uqs": w_uq_sw.astype(BF16),
        "w_dkv": w_dkv_all.astype(BF16), "kvg": kv_norm_g.reshape(1, -1),
        "w_uk": w_uk.astype(BF16), "w_uv": w_uv.astype(BF16),
        "w_o": w_o_pad.astype(BF16),
    }


def _rope_inv_freq_row():
    inv_freq = ROPE_THETA ** (-jnp.arange(0, QK_ROPE, 2, dtype=F32) / QK_ROPE)
    return jnp.tile(inv_freq, LANES // inv_freq.shape[0]).reshape(1, LANES)


def kernel(x, c, positions, ada_w, ada_b, norm_g, mla_w_dq, mla_q_norm_g, mla_w_uq, mla_w_dkv, mla_kv_norm_g, mla_w_ukv, mla_w_o, conv_w_pw1, conv_b_pw1, conv_w_dw, conv_b_dw, conv_ln_g, conv_ln_b, conv_w_pw2, conv_b_pw2, pool_w, pool_b, pool_scale, ffn_w1, ffn_w2):
    batch, seq, d = x.shape
    assert d == D_MODEL and seq % TOKEN_TILE == 0 and seq % ATTN_TILE == 0
    assert ATTN_TILE % CHUNK == 0 and TOKEN_TILE % CONV_ROWS == 0

    mod = _ada_mod(c, ada_w, ada_b)
    pos3 = positions.reshape(batch, seq, 1)
    invf = _rope_inv_freq_row()
    row = lambda v: v.reshape(1, -1)

    for i in range(DEPTH):
        kind = i % N_MIXERS
        j = i // N_MIXERS
        w1 = ffn_w1[i].astype(BF16)
        w2 = ffn_w2[i].astype(BF16)
        if kind == 0:
            w = _mla_weights(mla_w_dq[j], mla_q_norm_g[j], mla_w_uq[j], mla_w_dkv[j],
                             mla_kv_norm_g[j], mla_w_ukv[j], mla_w_o[j])
            q, k, v = _mla_proj(x, pos3, mod[i], norm_g[i], invf, w)
            o = _attention(q, k, v)
            x = _mla_out(x, o, mod[i], norm_g[i], w["w_o"], w1, w2)
        elif kind == 1:
            w = {
                "w_pw1": conv_w_pw1[j].astype(BF16), "b_pw1": row(conv_b_pw1[j]),
                "w_dw": conv_w_dw[j], "b_dw": row(conv_b_dw[j]),
                "ln_g": row(conv_ln_g[j]), "ln_b": row(conv_ln_b[j]),
                "w_pw2": conv_w_pw2[j].astype(BF16), "b_pw2": row(conv_b_pw2[j]),
            }
            x = _conv_layer(x, mod[i], norm_g[i], w, w1, w2)
        else:
            x = _pool_layer(x, mod[i], norm_g[i], pool_w[j].astype(BF16), pool_b[j],
                            row(pool_scale[j]), w1, w2)
    return x
```

```python
import functools
import math

import jax
import jax.numpy as jnp
from jax import lax
from jax.experimental import pallas as pl
from jax.experimental.pallas import tpu as pltpu

D_MODEL = 1024
DEPTH = 4
CHUNK = 64
N_MIXERS = 3
MLA_HEADS = 16
QK_NOPE = 64
QK_ROPE = 32
V_HEAD = 64
Q_LORA = 384
KV_LORA = 256
ROPE_THETA = 10000.0
CONV_WIDTH = 31
POOL_WINDOWS = (2, 4, 8, 16)
D_FF = 4 * D_MODEL
NORM_EPS = 1e-6
NEG_INF = -1e30

LANES = 128
HEAD_PAD = LANES
ROPE_LANE0 = QK_NOPE
ONES_ROW = V_HEAD
VMEM_LIMIT_BYTES = 56 * 1024 * 1024

TOKEN_TILE = 512
FF_CHUNK = 1024
ATTN_TILE = 512
ATTN_HEADS_PER_STEP = 2
CONV_ROWS = 32
HALO = 32
ADA_COLS = 1536

F32 = jnp.float32
BF16 = jnp.bfloat16
_NT_DIMS = (((1,), (1,)), ((), ()))


def _const_spec(shape):
    zeros = (0,) * len(shape)
    return pl.BlockSpec(shape, lambda *_: zeros, pipeline_mode=pl.Buffered(1))


def _params(*semantics):
    return pltpu.CompilerParams(dimension_semantics=semantics,
                                vmem_limit_bytes=VMEM_LIMIT_BYTES)


def _dot(a, b):
    return jnp.dot(a, b, preferred_element_type=F32)


def _rms(x, g):
    return x * lax.rsqrt(jnp.mean(x * x, axis=-1, keepdims=True) + NORM_EPS) * g


def _ffn_sublayer(x, mod, ng, w1_ref, w2_ref):
    h = (_rms(x, ng[2:3]) * (1.0 + mod[4:5]) + mod[3:4]).astype(BF16)
    y = jnp.zeros(x.shape, F32)
    for c in range(D_FF // FF_CHUNK):
        cols = slice(c * FF_CHUNK, (c + 1) * FF_CHUNK)
        a = jnp.maximum(_dot(h, w1_ref[:, cols]), 0.0)
        y = y + _dot((a * a).astype(BF16), w2_ref[cols, :])
    return x + mod[5:6] * _rms(y, ng[3:4])


def _ada_kernel(c_ref, w_ref, b_ref, o_ref):
    c = c_ref[...]
    c_act = (c * jax.nn.sigmoid(c)).astype(BF16)
    o_ref[...] = _dot(c_act, w_ref[...].astype(BF16)) + b_ref[...]


def _ada_mod(c, ada_w, ada_b):
    batch = c.shape[0]
    n_cols = ada_w.shape[-1]
    out = pl.pallas_call(
        _ada_kernel,
        grid=(DEPTH, n_cols // ADA_COLS),
        in_specs=[
            pl.BlockSpec((batch, D_MODEL), lambda l, n: (0, 0)),
            pl.BlockSpec((None, D_MODEL, ADA_COLS), lambda l, n: (l, 0, n)),
            pl.BlockSpec((None, 1, ADA_COLS), lambda l, n: (l, 0, n)),
        ],
        out_specs=pl.BlockSpec((None, batch, ADA_COLS), lambda l, n: (l, 0, n)),
        out_shape=jax.ShapeDtypeStruct((DEPTH, batch, n_cols), F32),
        compiler_params=_params("arbitrary", "arbitrary"),
        name="ada_mod",
    )(c, ada_w, ada_b.reshape(DEPTH, 1, n_cols))
    return out.reshape(DEPTH, batch, 6, D_MODEL)


def _mla_proj_kernel(x_ref, pos_ref, mod_ref, ng_ref, invf_ref, w_dq_ref, qg_ref, w_uq_ref,
                     w_uqs_ref, w_dkv_ref, kvg_ref, w_uk_ref, w_uvt_ref,
                     q_ref, k_ref, vt_ref):
    mod = mod_ref[...]
    ng = ng_ref[...]
    h = (_rms(x_ref[...], ng[0:1]) * (1.0 + mod[1:2]) + mod[0:1]).astype(BF16)

    ang = pos_ref[...].astype(F32) * invf_ref[...]
    lane = lax.broadcasted_iota(jnp.int32, (1, LANES), 1)
    first_half = (lane >= ROPE_LANE0) & (lane < ROPE_LANE0 + QK_ROPE // 2)
    second_half = (lane >= ROPE_LANE0 + QK_ROPE // 2) & (lane < ROPE_LANE0 + QK_ROPE)
    sin = jnp.sin(ang)
    cos_t = jnp.where(first_half | second_half, jnp.cos(ang), 1.0)
    sin_t = jnp.where(first_half, -sin, jnp.where(second_half, sin, 0.0))

    cq = _rms(_dot(h, w_dq_ref[...]), qg_ref[...]).astype(BF16)
    ckv_all = _dot(h, w_dkv_ref[...])
    ckv = _rms(ckv_all[:, :KV_LORA], kvg_ref[...]).astype(BF16)
    k_rope = (ckv_all[:, KV_LORA:KV_LORA + LANES] * cos_t
              + ckv_all[:, KV_LORA + LANES:] * sin_t)

    scale = math.log2(math.e) / math.sqrt(QK_NOPE + QK_ROPE)
    group = 4 * HEAD_PAD
    row_in_head = lax.broadcasted_iota(jnp.int32, (group, 1), 0) % HEAD_PAD
    ones_row = jnp.where(row_in_head == ONES_ROW, 1.0, 0.0)
    for g in range(MLA_HEADS * HEAD_PAD // group):
        cols = slice(g * group, (g + 1) * group)
        q = _dot(cq, w_uq_ref[:, cols])
        q_sw = _dot(cq, w_uqs_ref[:, cols])
        k = _dot(ckv, w_uk_ref[:, cols])
        for j in range(group // HEAD_PAD):
            sl = slice(j * HEAD_PAD, (j + 1) * HEAD_PAD)
            out = slice(g * group + j * HEAD_PAD, g * group + (j + 1) * HEAD_PAD)
            q_ref[:, out] = ((q[:, sl] * cos_t + q_sw[:, sl] * sin_t) * scale).astype(BF16)
            k_ref[:, out] = (k[:, sl] + k_rope).astype(BF16)
        vt = lax.dot_general(w_uvt_ref[cols, :], ckv, _NT_DIMS, preferred_element_type=F32)
        vt_ref[cols, :] = (vt + ones_row).astype(BF16)


def _mla_proj(x, pos3, mod, ng, invf, w):
    batch, seq, _ = x.shape
    tm = TOKEN_TILE
    wide = MLA_HEADS * HEAD_PAD
    tile = lambda width: pl.BlockSpec((None, tm, width), lambda b, t: (b, t, 0))
    out_sds = jax.ShapeDtypeStruct((batch, seq, wide), BF16)
    vt_sds = jax.ShapeDtypeStruct((batch, seq // tm, wide, tm), BF16)
    vt_spec = pl.BlockSpec((None, None, wide, tm), lambda b, t: (b, t, 0, 0))
    return pl.pallas_call(
        _mla_proj_kernel,
        grid=(batch, seq // tm),
        in_specs=[
            tile(D_MODEL),
            tile(1),
            pl.BlockSpec((None, 6, D_MODEL), lambda b, t: (b, 0, 0)),
            _const_spec((4, D_MODEL)),
            _const_spec((1, LANES)),
            _const_spec(w["w_dq"].shape),
            _const_spec(w["qg"].shape),
            _const_spec(w["w_uq"].shape),
            _const_spec(w["w_uqs"].shape),
            _const_spec(w["w_dkv"].shape),
            _const_spec(w["kvg"].shape),
            _const_spec(w["w_uk"].shape),
            _const_spec(w["w_uvt"].shape),
        ],
        out_specs=[tile(wide), tile(wide), vt_spec],
        out_shape=[out_sds, out_sds, vt_sds],
        compiler_params=_params("arbitrary", "arbitrary"),
        name="mla_proj",
    )(x, pos3, mod, ng, invf, w["w_dq"], w["qg"], w["w_uq"], w["w_uqs"], w["w_dkv"], w["kvg"],
      w["w_uk"], w["w_uvt"])


def _attn_kernel(q_ref, k_ref, vt_ref, o_ref, m_sc, acc_sc):
    t = ATTN_TILE
    i = pl.program_id(2)
    m_sc[...] = jnp.full(m_sc.shape, NEG_INF, F32)
    acc_sc[...] = jnp.zeros(acc_sc.shape, F32)

    def block(j, masked):
        rows = pl.ds(pl.multiple_of(j * t, t), t)
        for hh in range(ATTN_HEADS_PER_STEP):
            head = slice(hh * HEAD_PAD, (hh + 1) * HEAD_PAD)
            s = lax.dot_general(k_ref[rows, head], q_ref[:, head], _NT_DIMS,
                                preferred_element_type=F32)
            if masked:
                k_chunk = lax.broadcasted_iota(jnp.int32, (t, t), 0) // CHUNK
                q_chunk = lax.broadcasted_iota(jnp.int32, (t, t), 1) // CHUNK
                s = jnp.where(k_chunk <= q_chunk, s, NEG_INF)
            m_old = m_sc[hh:hh + 1, :]
            m_new = jnp.maximum(m_old, jnp.max(s, axis=0, keepdims=True))
            p = jnp.exp2(s - m_new).astype(BF16)
            acc_sc[head, :] = jnp.exp2(m_old - m_new) * acc_sc[head, :] + _dot(vt_ref[j, head, :], p)
            m_sc[hh:hh + 1, :] = m_new

    def body(j, carry):
        block(j, masked=False)
        return carry

    lax.fori_loop(0, i, body, 0)
    block(i, masked=True)
    for hh in range(ATTN_HEADS_PER_STEP):
        head = slice(hh * HEAD_PAD, (hh + 1) * HEAD_PAD)
        acc = acc_sc[head, :]
        o_ref[:, head] = (acc / acc[ONES_ROW:ONES_ROW + 1, :]).T.astype(BF16)


def _attention(q, k, vt):
    batch, seq, _ = q.shape
    t = ATTN_TILE
    n = ATTN_HEADS_PER_STEP
    width = n * HEAD_PAD
    return pl.pallas_call(
        _attn_kernel,
        grid=(batch, MLA_HEADS // n, seq // t),
        in_specs=[
            pl.BlockSpec((None, t, width), lambda b, h, i: (b, i, h)),
            pl.BlockSpec((None, seq, width), lambda b, h, i: (b, 0, h)),
            pl.BlockSpec((None, seq // t, width, t), lambda b, h, i: (b, 0, h, 0)),
        ],
        out_specs=pl.BlockSpec((None, t, width), lambda b, h, i: (b, i, h)),
        out_shape=jax.ShapeDtypeStruct(q.shape, BF16),
        scratch_shapes=[pltpu.VMEM((n, t), F32), pltpu.VMEM((width, t), F32)],
        compiler_params=_params("arbitrary", "arbitrary", "arbitrary"),
        name="mla_attention",
    )(q, k, vt)


def _mla_out_kernel(x_ref, o_ref, mod_ref, ng_ref, w_o_ref, w1_ref, w2_ref, out_ref):
    mod = mod_ref[...]
    ng = ng_ref[...]
    y = _dot(o_ref[...], w_o_ref[...])
    x = x_ref[...] + mod[2:3] * _rms(y, ng[1:2])
    out_ref[...] = _ffn_sublayer(x, mod, ng, w1_ref, w2_ref)


def _mla_out(x, o, mod, ng, w_o, w1, w2):
    batch, seq, _ = x.shape
    tm = TOKEN_TILE
    tile = lambda width: pl.BlockSpec((None, tm, width), lambda b, t: (b, t, 0))
    return pl.pallas_call(
        _mla_out_kernel,
        grid=(batch, seq // tm),
        in_specs=[
            tile(D_MODEL),
            tile(o.shape[-1]),
            pl.BlockSpec((None, 6, D_MODEL), lambda b, t: (b, 0, 0)),
            _const_spec((4, D_MODEL)),
            _const_spec(w_o.shape),
            _const_spec(w1.shape),
            _const_spec(w2.shape),
        ],
        out_specs=tile(D_MODEL),
        out_shape=jax.ShapeDtypeStruct(x.shape, F32),
        compiler_params=_params("arbitrary", "arbitrary"),
        name="mla_out_ffn",
    )(x, o, mod, ng, w_o, w1, w2)


def _conv_kernel(x_ref, mod_ref, ng_ref, w_pw1_ref, b_pw1_ref, w_dw_ref, b_dw_ref, ln_g_ref,
                 ln_b_ref, w_pw2_ref, b_pw2_ref, w1_ref, w2_ref, out_ref, u_sc, c_sc):
    tm = TOKEN_TILE
    mod = mod_ref[...]
    ng = ng_ref[...]
    x = x_ref[...]
    h = (_rms(x, ng[0:1]) * (1.0 + mod[1:2]) + mod[0:1]).astype(BF16)
    a = _dot(h, w_pw1_ref[...]) + b_pw1_ref[...]
    u = a[:, :D_MODEL] * jax.nn.sigmoid(a[:, D_MODEL:])

    @pl.when(pl.program_id(1) == 0)
    def _():
        u_sc[0:HALO, :] = jnp.zeros((HALO, D_MODEL), F32)

    @pl.when(pl.program_id(1) != 0)
    def _():
        u_sc[0:HALO, :] = u_sc[tm:tm + HALO, :]

    u_sc[HALO:HALO + tm, :] = u

    first_tap = HALO - (CONV_WIDTH - 1)
    w_dw = w_dw_ref[...]
    b_dw = b_dw_ref[...]

    for r0 in range(0, tm, CONV_ROWS):
        acc = jnp.broadcast_to(b_dw, (CONV_ROWS, D_MODEL))
        for j in range(CONV_WIDTH):
            acc = acc + u_sc[r0 + first_tap + j:r0 + first_tap + j + CONV_ROWS, :] * w_dw[j:j + 1, :]
        c_sc[r0:r0 + CONV_ROWS, :] = acc

    cv = c_sc[...]
    mu = jnp.mean(cv, axis=-1, keepdims=True)
    xc = cv - mu
    ln = xc * lax.rsqrt(jnp.mean(xc * xc, axis=-1, keepdims=True) + NORM_EPS)
    ln = ln * ln_g_ref[...] + ln_b_ref[...]
    act = (ln * jax.nn.sigmoid(ln)).astype(BF16)
    y = _dot(act, w_pw2_ref[...]) + b_pw2_ref[...]
    x = x + mod[2:3] * _rms(y, ng[1:2])
    out_ref[...] = _ffn_sublayer(x, mod, ng, w1_ref, w2_ref)


def _conv_layer(x, mod, ng, w, w1, w2):
    batch, seq, _ = x.shape
    tm = TOKEN_TILE
    tile = pl.BlockSpec((None, tm, D_MODEL), lambda b, t: (b, t, 0))
    names = ("w_pw1", "b_pw1", "w_dw", "b_dw", "ln_g", "ln_b", "w_pw2", "b_pw2")
    return pl.pallas_call(
        _conv_kernel,
        grid=(batch, seq // tm),
        in_specs=[tile,
                  pl.BlockSpec((None, 6, D_MODEL), lambda b, t: (b, 0, 0)),
                  _const_spec((4, D_MODEL))]
                 + [_const_spec(w[n].shape) for n in names]
                 + [_const_spec(w1.shape), _const_spec(w2.shape)],
        out_specs=tile,
        out_shape=jax.ShapeDtypeStruct(x.shape, F32),
        scratch_shapes=[pltpu.VMEM((HALO + tm, D_MODEL), F32), pltpu.VMEM((tm, D_MODEL), F32)],
        compiler_params=_params("arbitrary", "arbitrary"),
        name="conv_ffn",
    )(x, mod, ng, *[w[n] for n in names], w1, w2)


def _pool_kernel(x_ref, mod_ref, ng_ref, w_ref, b_ref, scale_ref, w1_ref, w2_ref, out_ref,
                 h_sc, s_a, s_b, y_sc):
    tm = TOKEN_TILE
    n_groups = len(POOL_WINDOWS)
    width = D_MODEL // n_groups
    mod = mod_ref[...]
    ng = ng_ref[...]
    x = x_ref[...]
    h = _rms(x, ng[0:1]) * (1.0 + mod[1:2]) + mod[0:1]

    @pl.when(pl.program_id(1) == 0)
    def _():
        h_sc[0:HALO, :] = jnp.zeros((HALO, D_MODEL), F32)

    @pl.when(pl.program_id(1) != 0)
    def _():
        h_sc[0:HALO, :] = h_sc[tm:tm + HALO, :]

    h_sc[HALO:HALO + tm, :] = h

    n = HALO + tm
    s_a[8:n, :] = h_sc[8:n, :] + h_sc[7:n - 1, :]
    s_b[16:n, width:] = s_a[16:n, width:] + s_a[14:n - 2, width:]
    s_a[24:n, 2 * width:] = s_b[24:n, 2 * width:] + s_b[20:n - 4, 2 * width:]
    s_b[32:n, 3 * width:] = s_a[32:n, 3 * width:] + s_a[24:n - 8, 3 * width:]

    t_idx = pl.program_id(1) * tm + lax.broadcasted_iota(jnp.int32, (tm, 1), 0)
    sums = (s_a, s_b, s_a, s_b)
    for g, win in enumerate(POOL_WINDOWS):
        cols = slice(g * width, (g + 1) * width)
        cnt = jnp.minimum(t_idx + 1, win).astype(F32)
        p = sums[g][HALO:n, cols] / cnt - h[:, cols]
        y_sc[:, cols] = _dot(p.astype(BF16), w_ref[g]) + b_ref[g:g + 1, :]

    y = y_sc[...] * scale_ref[...]
    x = x + mod[2:3] * _rms(y, ng[1:2])
    out_ref[...] = _ffn_sublayer(x, mod, ng, w1_ref, w2_ref)


def _pool_layer(x, mod, ng, w, b, scale, w1, w2):
    batch, seq, _ = x.shape
    tm = TOKEN_TILE
    tile = pl.BlockSpec((None, tm, D_MODEL), lambda b_, t: (b_, t, 0))
    halo_buf = pltpu.VMEM((HALO + tm, D_MODEL), F32)
    return pl.pallas_call(
        _pool_kernel,
        grid=(batch, seq // tm),
        in_specs=[tile,
                  pl.BlockSpec((None, 6, D_MODEL), lambda b_, t: (b_, 0, 0)),
                  _const_spec((4, D_MODEL)),
                  _const_spec(w.shape), _const_spec(b.shape), _const_spec(scale.shape),
                  _const_spec(w1.shape), _const_spec(w2.shape)],
        out_specs=tile,
        out_shape=jax.ShapeDtypeStruct(x.shape, F32),
        scratch_shapes=[halo_buf, halo_buf, halo_buf, pltpu.VMEM((tm, D_MODEL), F32)],
        compiler_params=_params("arbitrary", "arbitrary"),
        name="pool_ffn",
    )(x, mod, ng, w, b, scale, w1, w2)


def _pad_heads(w, per_head, lane0=0):
    k = w.shape[0]
    w = w.reshape(k, MLA_HEADS, per_head)
    w = jnp.pad(w, ((0, 0), (0, 0), (lane0, HEAD_PAD - lane0 - per_head)))
    return w.reshape(k, MLA_HEADS * HEAD_PAD)


def _swap_halves(w):
    half = w.shape[-1] // 2
    return jnp.concatenate([w[..., half:], w[..., :half]], axis=-1)


def _mla_weights(w_dq, q_norm_g, w_uq, w_dkv, kv_norm_g, w_ukv, w_o):
    uq = w_uq.reshape(Q_LORA, MLA_HEADS, QK_NOPE + QK_ROPE)
    uq_nope = uq[..., :QK_NOPE].reshape(Q_LORA, -1)
    uq_rope = uq[..., QK_NOPE:]
    rope_cols = lambda w3: _pad_heads(w3.reshape(Q_LORA, -1), QK_ROPE, ROPE_LANE0)
    w_uq_pad = _pad_heads(uq_nope, QK_NOPE) + rope_cols(uq_rope)
    w_uq_sw = rope_cols(_swap_halves(uq_rope))

    dkv_rope = w_dkv[:, KV_LORA:]
    pad_rope = lambda w: jnp.pad(w, ((0, 0), (ROPE_LANE0, LANES - ROPE_LANE0 - QK_ROPE)))
    w_dkv_all = jnp.concatenate(
        [w_dkv[:, :KV_LORA], pad_rope(dkv_rope), pad_rope(_swap_halves(dkv_rope))], axis=-1)

    ukv = w_ukv.reshape(KV_LORA, MLA_HEADS, QK_NOPE + V_HEAD)
    w_uk = _pad_heads(ukv[..., :QK_NOPE].reshape(KV_LORA, -1), QK_NOPE)
    w_uv = _pad_heads(ukv[..., QK_NOPE:].reshape(KV_LORA, -1), V_HEAD)

    o = w_o.reshape(MLA_HEADS, V_HEAD, D_MODEL)
    w_o_pad = jnp.pad(o, ((0, 0), (0, HEAD_PAD - V_HEAD), (0, 0))).reshape(-1, D_MODEL)
    return {
        "w_dq": w_dq.astype(BF16), "qg": q_norm_g.reshape(1, -1),
        "w_uq": w_uq_pad.astype(BF16), "w_uqs": w_uq_sw.astype(BF16),
        "w_dkv": w_dkv_all.astype(BF16), "kvg": kv_norm_g.reshape(1, -1),
        "w_uk": w_uk.astype(BF16), "w_uvt": w_uv.T.astype(BF16),
        "w_o": w_o_pad.astype(BF16),
    }


def _rope_inv_freq_row():
    inv_freq = ROPE_THETA ** (-jnp.arange(0, QK_ROPE, 2, dtype=F32) / QK_ROPE)
    return jnp.tile(inv_freq, LANES // inv_freq.shape[0]).reshape(1, LANES)


def kernel(x, c, positions, ada_w, ada_b, norm_g, mla_w_dq, mla_q_norm_g, mla_w_uq, mla_w_dkv, mla_kv_norm_g, mla_w_ukv, mla_w_o, conv_w_pw1, conv_b_pw1, conv_w_dw, conv_b_dw, conv_ln_g, conv_ln_b, conv_w_pw2, conv_b_pw2, pool_w, pool_b, pool_scale, ffn_w1, ffn_w2):
    batch, seq, d = x.shape
    assert d == D_MODEL and seq % TOKEN_TILE == 0 and seq % ATTN_TILE == 0
    assert ATTN_TILE % CHUNK == 0 and TOKEN_TILE % CONV_ROWS == 0
    assert TOKEN_TILE == ATTN_TILE

    mod = _ada_mod(c, ada_w, ada_b)
    pos3 = positions.reshape(batch, seq, 1)
    invf = _rope_inv_freq_row()
    row = lambda v: v.reshape(1, -1)

    for i in range(DEPTH):
        kind = i % N_MIXERS
        j = i // N_MIXERS
        w1 = ffn_w1[i].astype(BF16)
        w2 = ffn_w2[i].astype(BF16)
        if kind == 0:
            w = _mla_weights(mla_w_dq[j], mla_q_norm_g[j], mla_w_uq[j], mla_w_dkv[j],
                             mla_kv_norm_g[j], mla_w_ukv[j], mla_w_o[j])
            q, k, vt = _mla_proj(x, pos3, mod[i], norm_g[i], invf, w)
            o = _attention(q, k, vt)
            x = _mla_out(x, o, mod[i], norm_g[i], w["w_o"], w1, w2)
        elif kind == 1:
            w = {
                "w_pw1": conv_w_pw1[j].astype(BF16), "b_pw1": row(conv_b_pw1[j]),
                "w_dw": conv_w_dw[j], "b_dw": row(conv_b_dw[j]),
                "ln_g": row(conv_ln_g[j]), "ln_b": row(conv_ln_b[j]),
                "w_pw2": conv_w_pw2[j].astype(BF16), "b_pw2": row(conv_b_pw2[j]),
            }
            x = _conv_layer(x, mod[i], norm_g[i], w, w1, w2)
        else:
            x = _pool_layer(x, mod[i], norm_g[i], pool_w[j].astype(BF16), pool_b[j],
                            row(pool_scale[j]), w1, w2)
    return x
```

```python
import functools
import math

import jax
import jax.numpy as jnp
from jax import lax
from jax.experimental import pallas as pl
from jax.experimental.pallas import tpu as pltpu

D_MODEL = 1024
DEPTH = 4
CHUNK = 64
N_MIXERS = 3
MLA_HEADS = 16
QK_NOPE = 64
QK_ROPE = 32
V_HEAD = 64
Q_LORA = 384
KV_LORA = 256
ROPE_THETA = 10000.0
CONV_WIDTH = 31
POOL_WINDOWS = (2, 4, 8, 16)
D_FF = 4 * D_MODEL
NORM_EPS = 1e-6
NEG_INF = -1e30

LANES = 128
HEAD_PAD = LANES
ROPE_LANE0 = QK_NOPE
MASK_LANE0 = QK_NOPE + QK_ROPE
ONES_ROW = V_HEAD
VMEM_LIMIT_BYTES = 56 * 1024 * 1024

TOKEN_TILE = 512
FF_CHUNK = 1024
ATTN_TILE = 512
ATTN_HEADS_PER_STEP = 2
CONV_ROWS = 32
HALO = 32
ADA_COLS = 1536

F32 = jnp.float32
BF16 = jnp.bfloat16
_NT_DIMS = (((1,), (1,)), ((), ()))


def _const_spec(shape):
    zeros = (0,) * len(shape)
    return pl.BlockSpec(shape, lambda *_: zeros, pipeline_mode=pl.Buffered(1))


def _params(*semantics):
    return pltpu.CompilerParams(dimension_semantics=semantics,
                                vmem_limit_bytes=VMEM_LIMIT_BYTES)


def _dot(a, b):
    return jnp.dot(a, b, preferred_element_type=F32)


def _rms(x, g):
    return x * lax.rsqrt(jnp.mean(x * x, axis=-1, keepdims=True) + NORM_EPS) * g


def _ffn_sublayer(x, mod, ng, w1_ref, w2_ref):
    h = (_rms(x, ng[2:3]) * (1.0 + mod[4:5]) + mod[3:4]).astype(BF16)
    y = jnp.zeros(x.shape, F32)
    for c in range(D_FF // FF_CHUNK):
        cols = slice(c * FF_CHUNK, (c + 1) * FF_CHUNK)
        a = jnp.maximum(_dot(h, w1_ref[:, cols]), 0.0)
        y = y + _dot((a * a).astype(BF16), w2_ref[cols, :])
    return x + mod[5:6] * _rms(y, ng[3:4])


def _ada_kernel(c_ref, w_ref, b_ref, o_ref):
    c = c_ref[...]
    c_act = (c * jax.nn.sigmoid(c)).astype(BF16)
    o_ref[...] = _dot(c_act, w_ref[...].astype(BF16)) + b_ref[...]


def _ada_mod(c, ada_w, ada_b):
    batch = c.shape[0]
    n_cols = ada_w.shape[-1]
    out = pl.pallas_call(
        _ada_kernel,
        grid=(DEPTH, n_cols // ADA_COLS),
        in_specs=[
            pl.BlockSpec((batch, D_MODEL), lambda l, n: (0, 0)),
            pl.BlockSpec((None, D_MODEL, ADA_COLS), lambda l, n: (l, 0, n)),
            pl.BlockSpec((None, 1, ADA_COLS), lambda l, n: (l, 0, n)),
        ],
        out_specs=pl.BlockSpec((None, batch, ADA_COLS), lambda l, n: (l, 0, n)),
        out_shape=jax.ShapeDtypeStruct((DEPTH, batch, n_cols), F32),
        compiler_params=_params("arbitrary", "arbitrary"),
        name="ada_mod",
    )(c, ada_w, ada_b.reshape(DEPTH, 1, n_cols))
    return out.reshape(DEPTH, batch, 6, D_MODEL)


def _mla_proj_kernel(x_ref, pos_ref, mod_ref, ng_ref, invf_ref, w_dq_ref, qg_ref, w_uq_ref,
                     w_uqs_ref, w_dkv_ref, kvg_ref, w_uk_ref, w_uvt_ref,
                     q_ref, k_ref, vt_ref):
    mod = mod_ref[...]
    ng = ng_ref[...]
    h = (_rms(x_ref[...], ng[0:1]) * (1.0 + mod[1:2]) + mod[0:1]).astype(BF16)

    ang = pos_ref[...].astype(F32) * invf_ref[...]
    lane = lax.broadcasted_iota(jnp.int32, (1, LANES), 1)
    first_half = (lane >= ROPE_LANE0) & (lane < ROPE_LANE0 + QK_ROPE // 2)
    second_half = (lane >= ROPE_LANE0 + QK_ROPE // 2) & (lane < ROPE_LANE0 + QK_ROPE)
    sin = jnp.sin(ang)
    cos_t = jnp.where(first_half | second_half, jnp.cos(ang), 1.0)
    sin_t = jnp.where(first_half, -sin, jnp.where(second_half, sin, 0.0))

    cq = _rms(_dot(h, w_dq_ref[...]), qg_ref[...]).astype(BF16)
    ckv_all = _dot(h, w_dkv_ref[...])
    ckv = _rms(ckv_all[:, :KV_LORA], kvg_ref[...]).astype(BF16)
    k_rope = (ckv_all[:, KV_LORA:KV_LORA + LANES] * cos_t
              + ckv_all[:, KV_LORA + LANES:] * sin_t)
    tile_rows = ckv_all.shape[0]
    chunk_in_block = (lax.broadcasted_iota(jnp.int32, (tile_rows, LANES), 0) // CHUNK) % (ATTN_TILE // CHUNK)
    chunk_lane = lax.broadcasted_iota(jnp.int32, (tile_rows, LANES), 1) - MASK_LANE0
    k_rope = k_rope + jnp.where(chunk_lane == chunk_in_block, 1.0, 0.0)

    scale = math.log2(math.e) / math.sqrt(QK_NOPE + QK_ROPE)
    group = 4 * HEAD_PAD
    row_in_head = lax.broadcasted_iota(jnp.int32, (group, 1), 0) % HEAD_PAD
    ones_row = jnp.where(row_in_head == ONES_ROW, 1.0, 0.0)
    for g in range(MLA_HEADS * HEAD_PAD // group):
        cols = slice(g * group, (g + 1) * group)
        q = _dot(cq, w_uq_ref[:, cols])
        q_sw = _dot(cq, w_uqs_ref[:, cols])
        k = _dot(ckv, w_uk_ref[:, cols])
        for j in range(group // HEAD_PAD):
            sl = slice(j * HEAD_PAD, (j + 1) * HEAD_PAD)
            out = slice(g * group + j * HEAD_PAD, g * group + (j + 1) * HEAD_PAD)
            q_ref[:, out] = ((q[:, sl] * cos_t + q_sw[:, sl] * sin_t) * scale).astype(BF16)
            k_ref[:, out] = (k[:, sl] + k_rope).astype(BF16)
        vt = lax.dot_general(w_uvt_ref[cols, :], ckv, _NT_DIMS, preferred_element_type=F32)
        vt_ref[cols, :] = (vt + ones_row).astype(BF16)


def _mla_proj(x, pos3, mod, ng, invf, w):
    batch, seq, _ = x.shape
    tm = TOKEN_TILE
    wide = MLA_HEADS * HEAD_PAD
    tile = lambda width: pl.BlockSpec((None, tm, width), lambda b, t: (b, t, 0))
    out_sds = jax.ShapeDtypeStruct((batch, seq, wide), BF16)
    vt_sds = jax.ShapeDtypeStruct((batch, seq // tm, wide, tm), BF16)
    vt_spec = pl.BlockSpec((None, None, wide, tm), lambda b, t: (b, t, 0, 0))
    return pl.pallas_call(
        _mla_proj_kernel,
        grid=(batch, seq // tm),
        in_specs=[
            tile(D_MODEL),
            tile(1),
            pl.BlockSpec((None, 6, D_MODEL), lambda b, t: (b, 0, 0)),
            _const_spec((4, D_MODEL)),
            _const_spec((1, LANES)),
            _const_spec(w["w_dq"].shape),
            _const_spec(w["qg"].shape),
            _const_spec(w["w_uq"].shape),
            _const_spec(w["w_uqs"].shape),
            _const_spec(w["w_dkv"].shape),
            _const_spec(w["kvg"].shape),
            _const_spec(w["w_uk"].shape),
            _const_spec(w["w_uvt"].shape),
        ],
        out_specs=[tile(wide), tile(wide), vt_spec],
        out_shape=[out_sds, out_sds, vt_sds],
        compiler_params=_params("arbitrary", "arbitrary"),
        name="mla_proj",
    )(x, pos3, mod, ng, invf, w["w_dq"], w["qg"], w["w_uq"], w["w_uqs"], w["w_dkv"], w["kvg"],
      w["w_uk"], w["w_uvt"])


def _next_block(i, j):
    wrap = j >= i
    return jnp.where(wrap, i + 1, i), jnp.where(wrap, 0, j + 1)


def _attn_kernel(q_ref, k_ref, vt_ref, ot_ref, s_sc, smax_sc, p_sc, alpha_sc, m_sc, acc_sc, feat_sc):
    t = ATTN_TILE
    heads = [slice(hh * HEAD_PAD, (hh + 1) * HEAD_PAD) for hh in range(ATTN_HEADS_PER_STEP)]
    n_tiles = q_ref.shape[0] // t
    n_blocks = n_tiles * (n_tiles + 1) // 2

    row_chunk = lax.broadcasted_iota(jnp.int32, (t, HEAD_PAD), 0) // CHUNK
    later_chunk = lax.broadcasted_iota(jnp.int32, (t, HEAD_PAD), 1) - MASK_LANE0
    masked_lane = (later_chunk > row_chunk) & (later_chunk < t // CHUNK)
    feat_sc[0] = jnp.zeros((t, HEAD_PAD), BF16)
    feat_sc[1] = jnp.where(masked_lane, NEG_INF, 0.0).astype(BF16)
    acc_sc[...] = jnp.zeros(acc_sc.shape, F32)

    def rows(idx):
        return pl.ds(pl.multiple_of(idx * t, t), t)

    def scores(blk, slot):
        i, j = blk
        feat = feat_sc[(i == j).astype(jnp.int32)]
        for hh, head in enumerate(heads):
            s = lax.dot_general(k_ref[rows(j), head], q_ref[rows(i), head] + feat,
                                _NT_DIMS, preferred_element_type=F32)
            s_sc[slot, hh] = s
            smax_sc[slot, hh:hh + 1, :] = jnp.max(s, axis=0, keepdims=True)

    def softmax(blk, slot):
        _, j = blk
        for hh in range(len(heads)):
            m_old = jnp.where(j == 0, NEG_INF, m_sc[hh:hh + 1, :])
            m_new = jnp.maximum(m_old, smax_sc[slot, hh:hh + 1, :])
            p_sc[slot, hh] = jnp.exp2(s_sc[slot, hh] - m_new).astype(BF16)
            alpha_sc[slot, hh:hh + 1, :] = jnp.exp2(m_old - m_new)
            m_sc[hh:hh + 1, :] = m_new

    def values(blk, slot):
        i, j = blk
        for hh, head in enumerate(heads):
            acc = (alpha_sc[slot, hh:hh + 1, :] * acc_sc[head, :]
                   + _dot(vt_ref[j, head, :], p_sc[slot, hh]))
            acc_sc[head, :] = acc
            ot_ref[i, head, :] = (acc * (1.0 / acc[ONES_ROW:ONES_ROW + 1, :])).astype(BF16)

    zero = jnp.int32(0)
    first = (zero, zero)
    second = _next_block(*first)
    scores(first, 0)
    softmax(first, 0)
    scores(second, 1)

    def two_ticks(_, blk):
        blk1 = _next_block(*blk)
        blk2 = _next_block(*blk1)
        blk3 = _next_block(*blk2)
        values(blk, 0)
        softmax(blk1, 1)
        scores(blk2, 0)
        values(blk1, 1)
        softmax(blk2, 0)
        scores(blk3, 1)
        return blk2

    assert n_blocks % 2 == 0
    blk = lax.fori_loop(0, n_blocks // 2 - 1, two_ticks, first)
    last = _next_block(*blk)
    values(blk, 0)
    softmax(last, 1)
    values(last, 1)


def _attention(q, k, vt):
    batch, seq, _ = q.shape
    t = ATTN_TILE
    n = ATTN_HEADS_PER_STEP
    width = n * HEAD_PAD
    seq_spec = pl.BlockSpec((None, seq, width), lambda b, h: (b, 0, h))
    transposed_spec = pl.BlockSpec((None, seq // t, width, t), lambda b, h: (b, 0, h, 0))
    return pl.pallas_call(
        _attn_kernel,
        grid=(batch, MLA_HEADS // n),
        in_specs=[seq_spec, seq_spec, transposed_spec],
        out_specs=transposed_spec,
        out_shape=jax.ShapeDtypeStruct(vt.shape, BF16),
        scratch_shapes=[
            pltpu.VMEM((2, n, t, t), F32),
            pltpu.VMEM((2, n, t), F32),
            pltpu.VMEM((2, n, t, t), BF16),
            pltpu.VMEM((2, n, t), F32),
            pltpu.VMEM((n, t), F32),
            pltpu.VMEM((width, t), F32),
            pltpu.VMEM((2, t, HEAD_PAD), BF16),
        ],
        compiler_params=_params("arbitrary", "arbitrary"),
        name="mla_attention",
    )(q, k, vt)


def _mla_out_kernel(x_ref, ot_ref, mod_ref, ng_ref, w_o_ref, w1_ref, w2_ref, out_ref):
    mod = mod_ref[...]
    ng = ng_ref[...]
    y = lax.dot_general(ot_ref[...], w_o_ref[...], (((0,), (0,)), ((), ())),
                        preferred_element_type=F32)
    x = x_ref[...] + mod[2:3] * _rms(y, ng[1:2])
    out_ref[...] = _ffn_sublayer(x, mod, ng, w1_ref, w2_ref)


def _mla_out(x, o, mod, ng, w_o, w1, w2):
    batch, seq, _ = x.shape
    tm = TOKEN_TILE
    tile = lambda width: pl.BlockSpec((None, tm, width), lambda b, t: (b, t, 0))
    return pl.pallas_call(
        _mla_out_kernel,
        grid=(batch, seq // tm),
        in_specs=[
            tile(D_MODEL),
            pl.BlockSpec((None, None, o.shape[2], tm), lambda b, t: (b, t, 0, 0)),
            pl.BlockSpec((None, 6, D_MODEL), lambda b, t: (b, 0, 0)),
            _const_spec((4, D_MODEL)),
            _const_spec(w_o.shape),
            _const_spec(w1.shape),
            _const_spec(w2.shape),
        ],
        out_specs=tile(D_MODEL),
        out_shape=jax.ShapeDtypeStruct(x.shape, F32),
        compiler_params=_params("arbitrary", "arbitrary"),
        name="mla_out_ffn",
    )(x, o, mod, ng, w_o, w1, w2)


def _conv_kernel(x_ref, mod_ref, ng_ref, w_pw1_ref, b_pw1_ref, w_dw_ref, b_dw_ref, ln_g_ref,
                 ln_b_ref, w_pw2_ref, b_pw2_ref, w1_ref, w2_ref, out_ref, u_sc, c_sc):
    tm = TOKEN_TILE
    mod = mod_ref[...]
    ng = ng_ref[...]
    x = x_ref[...]
    h = (_rms(x, ng[0:1]) * (1.0 + mod[1:2]) + mod[0:1]).astype(BF16)
    a = _dot(h, w_pw1_ref[...]) + b_pw1_ref[...]
    u = a[:, :D_MODEL] * jax.nn.sigmoid(a[:, D_MODEL:])

    @pl.when(pl.program_id(1) == 0)
    def _():
        u_sc[0:HALO, :] = jnp.zeros((HALO, D_MODEL), F32)

    @pl.when(pl.program_id(1) != 0)
    def _():
        u_sc[0:HALO, :] = u_sc[tm:tm + HALO, :]

    u_sc[HALO:HALO + tm, :] = u

    first_tap = HALO - (CONV_WIDTH - 1)
    w_dw = w_dw_ref[...]
    b_dw = b_dw_ref[...]

    for r0 in range(0, tm, CONV_ROWS):
        acc = jnp.broadcast_to(b_dw, (CONV_ROWS, D_MODEL))
        for j in range(CONV_WIDTH):
            acc = acc + u_sc[r0 + first_tap + j:r0 + first_tap + j + CONV_ROWS, :] * w_dw[j:j + 1, :]
        c_sc[r0:r0 + CONV_ROWS, :] = acc

    cv = c_sc[...]
    mu = jnp.mean(cv, axis=-1, keepdims=True)
    xc = cv - mu
    ln = xc * lax.rsqrt(jnp.mean(xc * xc, axis=-1, keepdims=True) + NORM_EPS)
    ln = ln * ln_g_ref[...] + ln_b_ref[...]
    act = (ln * jax.nn.sigmoid(ln)).astype(BF16)
    y = _dot(act, w_pw2_ref[...]) + b_pw2_ref[...]
    x = x + mod[2:3] * _rms(y, ng[1:2])
    out_ref[...] = _ffn_sublayer(x, mod, ng, w1_ref, w2_ref)


def _conv_layer(x, mod, ng, w, w1, w2):
    batch, seq, _ = x.shape
    tm = TOKEN_TILE
    tile = pl.BlockSpec((None, tm, D_MODEL), lambda b, t: (b, t, 0))
    names = ("w_pw1", "b_pw1", "w_dw", "b_dw", "ln_g", "ln_b", "w_pw2", "b_pw2")
    return pl.pallas_call(
        _conv_kernel,
        grid=(batch, seq // tm),
        in_specs=[tile,
                  pl.BlockSpec((None, 6, D_MODEL), lambda b, t: (b, 0, 0)),
                  _const_spec((4, D_MODEL))]
                 + [_const_spec(w[n].shape) for n in names]
                 + [_const_spec(w1.shape), _const_spec(w2.shape)],
        out_specs=tile,
        out_shape=jax.ShapeDtypeStruct(x.shape, F32),
        scratch_shapes=[pltpu.VMEM((HALO + tm, D_MODEL), F32), pltpu.VMEM((tm, D_MODEL), F32)],
        compiler_params=_params("arbitrary", "arbitrary"),
        name="conv_ffn",
    )(x, mod, ng, *[w[n] for n in names], w1, w2)


def _pool_kernel(x_ref, mod_ref, ng_ref, w_ref, b_ref, scale_ref, w1_ref, w2_ref, out_ref,
                 h_sc, s_a, s_b, y_sc):
    tm = TOKEN_TILE
    n_groups = len(POOL_WINDOWS)
    width = D_MODEL // n_groups
    mod = mod_ref[...]
    ng = ng_ref[...]
    x = x_ref[...]
    h = _rms(x, ng[0:1]) * (1.0 + mod[1:2]) + mod[0:1]

    @pl.when(pl.program_id(1) == 0)
    def _():
        h_sc[0:HALO, :] = jnp.zeros((HALO, D_MODEL), F32)

    @pl.when(pl.program_id(1) != 0)
    def _():
        h_sc[0:HALO, :] = h_sc[tm:tm + HALO, :]

    h_sc[HALO:HALO + tm, :] = h

    n = HALO + tm
    s_a[8:n, :] = h_sc[8:n, :] + h_sc[7:n - 1, :]
    s_b[16:n, width:] = s_a[16:n, width:] + s_a[14:n - 2, width:]
    s_a[24:n, 2 * width:] = s_b[24:n, 2 * width:] + s_b[20:n - 4, 2 * width:]
    s_b[32:n, 3 * width:] = s_a[32:n, 3 * width:] + s_a[24:n - 8, 3 * width:]

    t_idx = pl.program_id(1) * tm + lax.broadcasted_iota(jnp.int32, (tm, 1), 0)
    sums = (s_a, s_b, s_a, s_b)
    for g, win in enumerate(POOL_WINDOWS):
        cols = slice(g * width, (g + 1) * width)
        cnt = jnp.minimum(t_idx + 1, win).astype(F32)
        p = sums[g][HALO:n, cols] / cnt - h[:, cols]
        y_sc[:, cols] = _dot(p.astype(BF16), w_ref[g]) + b_ref[g:g + 1, :]

    y = y_sc[...] * scale_ref[...]
    x = x + mod[2:3] * _rms(y, ng[1:2])
    out_ref[...] = _ffn_sublayer(x, mod, ng, w1_ref, w2_ref)


def _pool_layer(x, mod, ng, w, b, scale, w1, w2):
    batch, seq, _ = x.shape
    tm = TOKEN_TILE
    tile = pl.BlockSpec((None, tm, D_MODEL), lambda b_, t: (b_, t, 0))
    halo_buf = pltpu.VMEM((HALO + tm, D_MODEL), F32)
    return pl.pallas_call(
        _pool_kernel,
        grid=(batch, seq // tm),
        in_specs=[tile,
                  pl.BlockSpec((None, 6, D_MODEL), lambda b_, t: (b_, 0, 0)),
                  _const_spec((4, D_MODEL)),
                  _const_spec(w.shape), _const_spec(b.shape), _const_spec(scale.shape),
                  _const_spec(w1.shape), _const_spec(w2.shape)],
        out_specs=tile,
        out_shape=jax.ShapeDtypeStruct(x.shape, F32),
        scratch_shapes=[halo_buf, halo_buf, halo_buf, pltpu.VMEM((tm, D_MODEL), F32)],
        compiler_params=_params("arbitrary", "arbitrary"),
        name="pool_ffn",
    )(x, mod, ng, w, b, scale, w1, w2)


def _pad_heads(w, per_head, lane0=0):
    k = w.shape[0]
    w = w.reshape(k, MLA_HEADS, per_head)
    w = jnp.pad(w, ((0, 0), (0, 0), (lane0, HEAD_PAD - lane0 - per_head)))
    return w.reshape(k, MLA_HEADS * HEAD_PAD)


def _swap_halves(w):
    half = w.shape[-1] // 2
    return jnp.concatenate([w[..., half:], w[..., :half]], axis=-1)


def _mla_weights(w_dq, q_norm_g, w_uq, w_dkv, kv_norm_g, w_ukv, w_o):
    uq = w_uq.reshape(Q_LORA, MLA_HEADS, QK_NOPE + QK_ROPE)
    uq_nope = uq[..., :QK_NOPE].reshape(Q_LORA, -1)
    uq_rope = uq[..., QK_NOPE:]
    rope_cols = lambda w3: _pad_heads(w3.reshape(Q_LORA, -1), QK_ROPE, ROPE_LANE0)
    w_uq_pad = _pad_heads(uq_nope, QK_NOPE) + rope_cols(uq_rope)
    w_uq_sw = rope_cols(_swap_halves(uq_rope))

    dkv_rope = w_dkv[:, KV_LORA:]
    pad_rope = lambda w: jnp.pad(w, ((0, 0), (ROPE_LANE0, LANES - ROPE_LANE0 - QK_ROPE)))
    w_dkv_all = jnp.concatenate(
        [w_dkv[:, :KV_LORA], pad_rope(dkv_rope), pad_rope(_swap_halves(dkv_rope))], axis=-1)

    ukv = w_ukv.reshape(KV_LORA, MLA_HEADS, QK_NOPE + V_HEAD)
    w_uk = _pad_heads(ukv[..., :QK_NOPE].reshape(KV_LORA, -1), QK_NOPE)
    w_uv = _pad_heads(ukv[..., QK_NOPE:].reshape(KV_LORA, -1), V_HEAD)

    o = w_o.reshape(MLA_HEADS, V_HEAD, D_MODEL)
    w_o_pad = jnp.pad(o, ((0, 0), (0, HEAD_PAD - V_HEAD), (0, 0))).reshape(-1, D_MODEL)
    return {
        "w_dq": w_dq.astype(BF16), "qg": q_norm_g.reshape(1, -1),
        "w_uq": w_uq_pad.astype(BF16), "w_uqs": w_uq_sw.astype(BF16),
        "w_dkv": w_dkv_all.astype(BF16), "kvg": kv_norm_g.reshape(1, -1),
        "w_uk": w_uk.astype(BF16), "w_uvt": w_uv.T.astype(BF16),
        "w_o": w_o_pad.astype(BF16),
    }


def _rope_inv_freq_row():
    inv_freq = ROPE_THETA ** (-jnp.arange(0, QK_ROPE, 2, dtype=F32) / QK_ROPE)
    return jnp.tile(inv_freq, LANES // inv_freq.shape[0]).reshape(1, LANES)


def kernel(x, c, positions, ada_w, ada_b, norm_g, mla_w_dq, mla_q_norm_g, mla_w_uq, mla_w_dkv, mla_kv_norm_g, mla_w_ukv, mla_w_o, conv_w_pw1, conv_b_pw1, conv_w_dw, conv_b_dw, conv_ln_g, conv_ln_b, conv_w_pw2, conv_b_pw2, pool_w, pool_b, pool_scale, ffn_w1, ffn_w2):
    batch, seq, d = x.shape
    assert d == D_MODEL and seq % TOKEN_TILE == 0 and seq % ATTN_TILE == 0
    assert ATTN_TILE % CHUNK == 0 and TOKEN_TILE % CONV_ROWS == 0
    assert TOKEN_TILE == ATTN_TILE

    mod = _ada_mod(c, ada_w, ada_b)
    pos3 = positions.reshape(batch, seq, 1)
    invf = _rope_inv_freq_row()
    row = lambda v: v.reshape(1, -1)

    for i in range(DEPTH):
        kind = i % N_MIXERS
        j = i // N_MIXERS
        w1 = ffn_w1[i].astype(BF16)
        w2 = ffn_w2[i].astype(BF16)
        if kind == 0:
            w = _mla_weights(mla_w_dq[j], mla_q_norm_g[j], mla_w_uq[j], mla_w_dkv[j],
                             mla_kv_norm_g[j], mla_w_ukv[j], mla_w_o[j])
            q, k, vt = _mla_proj(x, pos3, mod[i], norm_g[i], invf, w)
            o = _attention(q, k, vt)
            x = _mla_out(x, o, mod[i], norm_g[i], w["w_o"], w1, w2)
        elif kind == 1:
            w = {
                "w_pw1": conv_w_pw1[j].astype(BF16), "b_pw1": row(conv_b_pw1[j]),
                "w_dw": conv_w_dw[j], "b_dw": row(conv_b_dw[j]),
                "ln_g": row(conv_ln_g[j]), "ln_b": row(conv_ln_b[j]),
                "w_pw2": conv_w_pw2[j].astype(BF16), "b_pw2": row(conv_b_pw2[j]),
            }
            x = _conv_layer(x, mod[i], norm_g[i], w, w1, w2)
        else:
            x = _pool_layer(x, mod[i], norm_g[i], pool_w[j].astype(BF16), pool_b[j],
                            row(pool_scale[j]), w1, w2)
    return x
```

```python
import functools
import math

import jax
import jax.numpy as jnp
from jax import lax
from jax.experimental import pallas as pl
from jax.experimental.pallas import tpu as pltpu

D_MODEL = 1024
DEPTH = 4
CHUNK = 64
N_MIXERS = 3
MLA_HEADS = 16
QK_NOPE = 64
QK_ROPE = 32
V_HEAD = 64
Q_LORA = 384
KV_LORA = 256
ROPE_THETA = 10000.0
CONV_WIDTH = 31
POOL_WINDOWS = (2, 4, 8, 16)
D_FF = 4 * D_MODEL
NORM_EPS = 1e-6
NEG_INF = -1e30

LANES = 128
HEAD_PAD = LANES
ROPE_LANE0 = QK_NOPE
MASK_LANE0 = QK_NOPE + QK_ROPE
ONES_ROW = V_HEAD
VMEM_LIMIT_BYTES = 56 * 1024 * 1024

TOKEN_TILE = 512
FF_CHUNK = 1024
ATTN_TILE = 512
ATTN_HEADS_PER_STEP = 2
SUBLANES = 8
CONV_ROWS = 64
CONV_COLS = 256
HALO = 32
ADA_COLS = 1536

F32 = jnp.float32
BF16 = jnp.bfloat16
_NT_DIMS = (((1,), (1,)), ((), ()))


def _const_spec(shape):
    zeros = (0,) * len(shape)
    return pl.BlockSpec(shape, lambda *_: zeros, pipeline_mode=pl.Buffered(1))


def _params(*semantics):
    return pltpu.CompilerParams(dimension_semantics=semantics,
                                vmem_limit_bytes=VMEM_LIMIT_BYTES)


def _dot(a, b):
    return jnp.dot(a, b, preferred_element_type=F32)


def _rms(x, g):
    return x * lax.rsqrt(jnp.mean(x * x, axis=-1, keepdims=True) + NORM_EPS) * g


def _ffn_sublayer(x, mod, ng, w1_ref, w2_ref):
    h = (_rms(x, ng[2:3]) * (1.0 + mod[4:5]) + mod[3:4]).astype(BF16)
    y = jnp.zeros(x.shape, F32)
    for c in range(D_FF // FF_CHUNK):
        cols = slice(c * FF_CHUNK, (c + 1) * FF_CHUNK)
        a = jnp.maximum(_dot(h, w1_ref[:, cols]), 0.0)
        y = y + _dot((a * a).astype(BF16), w2_ref[cols, :])
    return x + mod[5:6] * _rms(y, ng[3:4])


def _ada_kernel(c_ref, w_ref, b_ref, o_ref):
    c = c_ref[...]
    c_act = (c * jax.nn.sigmoid(c)).astype(BF16)
    o_ref[...] = _dot(c_act, w_ref[...].astype(BF16)) + b_ref[...]


def _ada_mod(c, ada_w, ada_b):
    batch = c.shape[0]
    n_cols = ada_w.shape[-1]
    out = pl.pallas_call(
        _ada_kernel,
        grid=(DEPTH, n_cols // ADA_COLS),
        in_specs=[
            pl.BlockSpec((batch, D_MODEL), lambda l, n: (0, 0)),
            pl.BlockSpec((None, D_MODEL, ADA_COLS), lambda l, n: (l, 0, n)),
            pl.BlockSpec((None, 1, ADA_COLS), lambda l, n: (l, 0, n)),
        ],
        out_specs=pl.BlockSpec((None, batch, ADA_COLS), lambda l, n: (l, 0, n)),
        out_shape=jax.ShapeDtypeStruct((DEPTH, batch, n_cols), F32),
        compiler_params=_params("arbitrary", "arbitrary"),
        name="ada_mod",
    )(c, ada_w, ada_b.reshape(DEPTH, 1, n_cols))
    return out.reshape(DEPTH, batch, 6, D_MODEL)


def _mla_proj_kernel(x_ref, pos_ref, mod_ref, ng_ref, invf_ref, w_dq_ref, qg_ref, w_uq_ref,
                     w_uqs_ref, w_dkv_ref, kvg_ref, w_uk_ref, w_uvt_ref,
                     q_ref, k_ref, vt_ref):
    mod = mod_ref[...]
    ng = ng_ref[...]
    h = (_rms(x_ref[...], ng[0:1]) * (1.0 + mod[1:2]) + mod[0:1]).astype(BF16)

    ang = pos_ref[...].astype(F32) * invf_ref[...]
    lane = lax.broadcasted_iota(jnp.int32, (1, LANES), 1)
    first_half = (lane >= ROPE_LANE0) & (lane < ROPE_LANE0 + QK_ROPE // 2)
    second_half = (lane >= ROPE_LANE0 + QK_ROPE // 2) & (lane < ROPE_LANE0 + QK_ROPE)
    sin = jnp.sin(ang)
    cos_t = jnp.where(first_half | second_half, jnp.cos(ang), 1.0)
    sin_t = jnp.where(first_half, -sin, jnp.where(second_half, sin, 0.0))

    cq = _rms(_dot(h, w_dq_ref[...]), qg_ref[...]).astype(BF16)
    ckv_all = _dot(h, w_dkv_ref[...])
    ckv = _rms(ckv_all[:, :KV_LORA], kvg_ref[...]).astype(BF16)
    k_rope = (ckv_all[:, KV_LORA:KV_LORA + LANES] * cos_t
              + ckv_all[:, KV_LORA + LANES:] * sin_t)
    tile_rows = ckv_all.shape[0]
    chunk_in_block = (lax.broadcasted_iota(jnp.int32, (tile_rows, LANES), 0) // CHUNK) % (ATTN_TILE // CHUNK)
    chunk_lane = lax.broadcasted_iota(jnp.int32, (tile_rows, LANES), 1) - MASK_LANE0
    k_rope = k_rope + jnp.where(chunk_lane == chunk_in_block, 1.0, 0.0)

    scale = math.log2(math.e) / math.sqrt(QK_NOPE + QK_ROPE)
    group = 4 * HEAD_PAD
    row_in_head = lax.broadcasted_iota(jnp.int32, (group, 1), 0) % HEAD_PAD
    ones_row = jnp.where(row_in_head == ONES_ROW, 1.0, 0.0)
    for g in range(MLA_HEADS * HEAD_PAD // group):
        cols = slice(g * group, (g + 1) * group)
        q = _dot(cq, w_uq_ref[:, cols])
        q_sw = _dot(cq, w_uqs_ref[:, cols])
        k = _dot(ckv, w_uk_ref[:, cols])
        for j in range(group // HEAD_PAD):
            sl = slice(j * HEAD_PAD, (j + 1) * HEAD_PAD)
            out = slice(g * group + j * HEAD_PAD, g * group + (j + 1) * HEAD_PAD)
            q_ref[:, out] = ((q[:, sl] * cos_t + q_sw[:, sl] * sin_t) * scale).astype(BF16)
            k_ref[:, out] = (k[:, sl] + k_rope).astype(BF16)
        vt = lax.dot_general(w_uvt_ref[cols, :], ckv, _NT_DIMS, preferred_element_type=F32)
        vt_ref[cols, :] = (vt + ones_row).astype(BF16)


def _mla_proj(x, pos3, mod, ng, invf, w):
    batch, seq, _ = x.shape
    tm = TOKEN_TILE
    wide = MLA_HEADS * HEAD_PAD
    tile = lambda width: pl.BlockSpec((None, tm, width), lambda b, t: (b, t, 0))
    out_sds = jax.ShapeDtypeStruct((batch, seq, wide), BF16)
    vt_sds = jax.ShapeDtypeStruct((batch, seq // tm, wide, tm), BF16)
    vt_spec = pl.BlockSpec((None, None, wide, tm), lambda b, t: (b, t, 0, 0))
    return pl.pallas_call(
        _mla_proj_kernel,
        grid=(batch, seq // tm),
        in_specs=[
            tile(D_MODEL),
            tile(1),
            pl.BlockSpec((None, 6, D_MODEL), lambda b, t: (b, 0, 0)),
            _const_spec((4, D_MODEL)),
            _const_spec((1, LANES)),
            _const_spec(w["w_dq"].shape),
            _const_spec(w["qg"].shape),
            _const_spec(w["w_uq"].shape),
            _const_spec(w["w_uqs"].shape),
            _const_spec(w["w_dkv"].shape),
            _const_spec(w["kvg"].shape),
            _const_spec(w["w_uk"].shape),
            _const_spec(w["w_uvt"].shape),
        ],
        out_specs=[tile(wide), tile(wide), vt_spec],
        out_shape=[out_sds, out_sds, vt_sds],
        compiler_params=_params("arbitrary", "arbitrary"),
        name="mla_proj",
    )(x, pos3, mod, ng, invf, w["w_dq"], w["qg"], w["w_uq"], w["w_uqs"], w["w_dkv"], w["kvg"],
      w["w_uk"], w["w_uvt"])


def _next_block(i, j):
    wrap = j >= i
    return jnp.where(wrap, i + 1, i), jnp.where(wrap, 0, j + 1)


def _attn_kernel(q_ref, k_ref, vt_ref, ot_ref, m_sc, acc_sc, feat_sc, *stage_sc):
    t = ATTN_TILE
    heads = [slice(hh * HEAD_PAD, (hh + 1) * HEAD_PAD) for hh in range(ATTN_HEADS_PER_STEP)]
    n_tiles = q_ref.shape[0] // t
    n_blocks = n_tiles * (n_tiles + 1) // 2
    n_heads = len(heads)
    per_kind = 2 * n_heads
    by_slot = lambda refs: [refs[slot * n_heads:(slot + 1) * n_heads] for slot in range(2)]
    s_sc, smax_sc, p_sc, alpha_sc = (by_slot(stage_sc[kind * per_kind:(kind + 1) * per_kind])
                                     for kind in range(4))

    row_chunk = lax.broadcasted_iota(jnp.int32, (t, HEAD_PAD), 0) // CHUNK
    later_chunk = lax.broadcasted_iota(jnp.int32, (t, HEAD_PAD), 1) - MASK_LANE0
    masked_lane = (later_chunk > row_chunk) & (later_chunk < t // CHUNK)
    feat_sc[0] = jnp.zeros((t, HEAD_PAD), BF16)
    feat_sc[1] = jnp.where(masked_lane, NEG_INF, 0.0).astype(BF16)
    acc_sc[...] = jnp.zeros(acc_sc.shape, F32)

    def rows(idx):
        return pl.ds(pl.multiple_of(idx * t, t), t)

    def scores(blk, slot):
        i, j = blk
        feat = feat_sc[(i == j).astype(jnp.int32)]
        for hh, head in enumerate(heads):
            s = lax.dot_general(k_ref[rows(j), head], q_ref[rows(i), head] + feat,
                                _NT_DIMS, preferred_element_type=F32)
            s_sc[slot][hh][...] = s
            smax_sc[slot][hh][...] = jnp.max(s, axis=0, keepdims=True)

    def softmax(blk, slot):
        _, j = blk
        for hh in range(len(heads)):
            m_old = jnp.where(j == 0, NEG_INF, m_sc[hh:hh + 1, :])
            m_new = jnp.maximum(m_old, smax_sc[slot][hh][...])
            p_sc[slot][hh][...] = jnp.exp2(s_sc[slot][hh][...] - m_new).astype(BF16)
            alpha_sc[slot][hh][...] = jnp.exp2(m_old - m_new)
            m_sc[hh:hh + 1, :] = m_new

    def values(blk, slot):
        i, j = blk
        for hh, head in enumerate(heads):
            acc = (alpha_sc[slot][hh][...] * acc_sc[head, :]
                   + _dot(vt_ref[j, head, :], p_sc[slot][hh][...]))
            acc_sc[head, :] = acc
            inv_sum = 1.0 / acc[ONES_ROW:ONES_ROW + 1, :]
            ot_ref[i, hh * V_HEAD:(hh + 1) * V_HEAD, :] = (acc[:V_HEAD, :] * inv_sum).astype(BF16)

    zero = jnp.int32(0)
    first = (zero, zero)
    second = _next_block(*first)
    scores(first, 0)
    softmax(first, 0)
    scores(second, 1)

    def two_ticks(_, blk):
        blk1 = _next_block(*blk)
        blk2 = _next_block(*blk1)
        blk3 = _next_block(*blk2)
        values(blk, 0)
        softmax(blk1, 1)
        scores(blk2, 0)
        values(blk1, 1)
        softmax(blk2, 0)
        scores(blk3, 1)
        return blk2

    assert n_blocks % 2 == 0
    blk = lax.fori_loop(0, n_blocks // 2 - 1, two_ticks, first)
    last = _next_block(*blk)
    values(blk, 0)
    softmax(last, 1)
    values(last, 1)


def _attention(q, k, vt):
    batch, seq, _ = q.shape
    t = ATTN_TILE
    n = ATTN_HEADS_PER_STEP
    width = n * HEAD_PAD
    seq_spec = pl.BlockSpec((None, seq, width), lambda b, h: (b, 0, h))
    transposed_spec = pl.BlockSpec((None, seq // t, width, t), lambda b, h: (b, 0, h, 0))
    return pl.pallas_call(
        _attn_kernel,
        grid=(batch, MLA_HEADS // n),
        in_specs=[seq_spec, seq_spec, transposed_spec],
        out_specs=pl.BlockSpec((None, seq // t, n * V_HEAD, t), lambda b, h: (b, 0, h, 0)),
        out_shape=jax.ShapeDtypeStruct((batch, seq // t, MLA_HEADS * V_HEAD, t), BF16),
        scratch_shapes=[
            pltpu.VMEM((n, t), F32),
            pltpu.VMEM((width, t), F32),
            pltpu.VMEM((2, t, HEAD_PAD), BF16),
        ]
        + [pltpu.VMEM((t, t), F32)] * (2 * n) + [pltpu.VMEM((1, t), F32)] * (2 * n)
        + [pltpu.VMEM((t, t), BF16)] * (2 * n) + [pltpu.VMEM((1, t), F32)] * (2 * n),
        compiler_params=_params("arbitrary", "arbitrary"),
        name="mla_attention",
    )(q, k, vt)


def _mla_out_kernel(x_ref, ot_ref, mod_ref, ng_ref, w_o_ref, w1_ref, w2_ref, out_ref):
    mod = mod_ref[...]
    ng = ng_ref[...]
    y = lax.dot_general(ot_ref[...], w_o_ref[...], (((0,), (0,)), ((), ())),
                        preferred_element_type=F32)
    x = x_ref[...] + mod[2:3] * _rms(y, ng[1:2])
    out_ref[...] = _ffn_sublayer(x, mod, ng, w1_ref, w2_ref)


def _mla_out(x, o, mod, ng, w_o, w1, w2):
    batch, seq, _ = x.shape
    tm = TOKEN_TILE
    tile = lambda width: pl.BlockSpec((None, tm, width), lambda b, t: (b, t, 0))
    return pl.pallas_call(
        _mla_out_kernel,
        grid=(batch, seq // tm),
        in_specs=[
            tile(D_MODEL),
            pl.BlockSpec((None, None, o.shape[2], tm), lambda b, t: (b, t, 0, 0)),
            pl.BlockSpec((None, 6, D_MODEL), lambda b, t: (b, 0, 0)),
            _const_spec((4, D_MODEL)),
            _const_spec(w_o.shape),
            _const_spec(w1.shape),
            _const_spec(w2.shape),
        ],
        out_specs=tile(D_MODEL),
        out_shape=jax.ShapeDtypeStruct(x.shape, F32),
        compiler_params=_params("arbitrary", "arbitrary"),
        name="mla_out_ffn",
    )(x, o, mod, ng, w_o, w1, w2)


def _conv_kernel(x_ref, mod_ref, ng_ref, w_pw1_ref, b_pw1_ref, w_dw_ref, b_dw_ref, ln_g_ref,
                 ln_b_ref, w_pw2_ref, b_pw2_ref, w1_ref, w2_ref, out_ref, u_sc, v_sc, c_sc):
    tm = TOKEN_TILE
    mod = mod_ref[...]
    ng = ng_ref[...]
    x = x_ref[...]
    h = (_rms(x, ng[0:1]) * (1.0 + mod[1:2]) + mod[0:1]).astype(BF16)
    a = _dot(h, w_pw1_ref[...]) + b_pw1_ref[...]
    u = a[:, :D_MODEL] * jax.nn.sigmoid(a[:, D_MODEL:])

    @pl.when(pl.program_id(1) == 0)
    def _():
        u_sc[0:HALO, :] = jnp.zeros((HALO, D_MODEL), F32)

    @pl.when(pl.program_id(1) != 0)
    def _():
        u_sc[0:HALO, :] = u_sc[tm:tm + HALO, :]

    u_sc[HALO:HALO + tm, :] = u

    first_tap = HALO - (CONV_WIDTH - 1)
    taps = [[] for _ in range(SUBLANES)]
    for j in range(CONV_WIDTH):
        taps[(first_tap + j) % SUBLANES].append(((first_tap + j) // SUBLANES, j))
    w_dw = w_dw_ref[...]
    b_dw = b_dw_ref[...]

    for c0 in range(0, D_MODEL, CONV_COLS):
        cols = slice(c0, c0 + CONV_COLS)

        def partial_sum(r, r0, nrows):
            return sum(u_sc[r0 + SUBLANES * a:r0 + SUBLANES * a + nrows, cols] * w_dw[j:j + 1, cols]
                       for a, j in taps[r])

        for r0 in range(0, tm + SUBLANES, CONV_ROWS):
            nrows = min(CONV_ROWS, tm + SUBLANES - r0)
            for r in range(1, SUBLANES):
                v_sc[r - 1, r0:r0 + nrows, :] = partial_sum(r, r0, nrows)
        for r0 in range(0, tm, CONV_ROWS):
            acc = b_dw[:, cols] + partial_sum(0, r0, CONV_ROWS)
            for r in range(1, SUBLANES):
                acc = acc + v_sc[r - 1, r0 + r:r0 + r + CONV_ROWS, :]
            c_sc[r0:r0 + CONV_ROWS, cols] = acc

    cv = c_sc[...]
    mu = jnp.mean(cv, axis=-1, keepdims=True)
    xc = cv - mu
    ln = xc * lax.rsqrt(jnp.mean(xc * xc, axis=-1, keepdims=True) + NORM_EPS)
    ln = ln * ln_g_ref[...] + ln_b_ref[...]
    act = (ln * jax.nn.sigmoid(ln)).astype(BF16)
    y = _dot(act, w_pw2_ref[...]) + b_pw2_ref[...]
    x = x + mod[2:3] * _rms(y, ng[1:2])
    out_ref[...] = _ffn_sublayer(x, mod, ng, w1_ref, w2_ref)


def _conv_layer(x, mod, ng, w, w1, w2):
    batch, seq, _ = x.shape
    tm = TOKEN_TILE
    tile = pl.BlockSpec((None, tm, D_MODEL), lambda b, t: (b, t, 0))
    names = ("w_pw1", "b_pw1", "w_dw", "b_dw", "ln_g", "ln_b", "w_pw2", "b_pw2")
    return pl.pallas_call(
        _conv_kernel,
        grid=(batch, seq // tm),
        in_specs=[tile,
                  pl.BlockSpec((None, 6, D_MODEL), lambda b, t: (b, 0, 0)),
                  _const_spec((4, D_MODEL))]
                 + [_const_spec(w[n].shape) for n in names]
                 + [_const_spec(w1.shape), _const_spec(w2.shape)],
        out_specs=tile,
        out_shape=jax.ShapeDtypeStruct(x.shape, F32),
        scratch_shapes=[pltpu.VMEM((HALO + tm, D_MODEL), F32),
                        pltpu.VMEM((SUBLANES - 1, tm + SUBLANES, CONV_COLS), F32),
                        pltpu.VMEM((tm, D_MODEL), F32)],
        compiler_params=_params("arbitrary", "arbitrary"),
        name="conv_ffn",
    )(x, mod, ng, *[w[n] for n in names], w1, w2)


def _pool_kernel(x_ref, mod_ref, ng_ref, w_ref, b_ref, scale_ref, w1_ref, w2_ref, out_ref,
                 h_sc, s_a, s_b, y_sc):
    tm = TOKEN_TILE
    n_groups = len(POOL_WINDOWS)
    width = D_MODEL // n_groups
    mod = mod_ref[...]
    ng = ng_ref[...]
    x = x_ref[...]
    h = _rms(x, ng[0:1]) * (1.0 + mod[1:2]) + mod[0:1]

    @pl.when(pl.program_id(1) == 0)
    def _():
        h_sc[0:HALO, :] = jnp.zeros((HALO, D_MODEL), F32)

    @pl.when(pl.program_id(1) != 0)
    def _():
        h_sc[0:HALO, :] = h_sc[tm:tm + HALO, :]

    h_sc[HALO:HALO + tm, :] = h

    n = HALO + tm
    s_a[8:n, :] = h_sc[8:n, :] + h_sc[7:n - 1, :]
    s_b[16:n, width:] = s_a[16:n, width:] + s_a[14:n - 2, width:]
    s_a[24:n, 2 * width:] = s_b[24:n, 2 * width:] + s_b[20:n - 4, 2 * width:]
    s_b[32:n, 3 * width:] = s_a[32:n, 3 * width:] + s_a[24:n - 8, 3 * width:]

    t_idx = pl.program_id(1) * tm + lax.broadcasted_iota(jnp.int32, (tm, 1), 0)
    sums = (s_a, s_b, s_a, s_b)
    for g, win in enumerate(POOL_WINDOWS):
        cols = slice(g * width, (g + 1) * width)
        cnt = jnp.minimum(t_idx + 1, win).astype(F32)
        p = sums[g][HALO:n, cols] / cnt - h[:, cols]
        y_sc[:, cols] = _dot(p.astype(BF16), w_ref[g]) + b_ref[g:g + 1, :]

    y = y_sc[...] * scale_ref[...]
    x = x + mod[2:3] * _rms(y, ng[1:2])
    out_ref[...] = _ffn_sublayer(x, mod, ng, w1_ref, w2_ref)


def _pool_layer(x, mod, ng, w, b, scale, w1, w2):
    batch, seq, _ = x.shape
    tm = TOKEN_TILE
    tile = pl.BlockSpec((None, tm, D_MODEL), lambda b_, t: (b_, t, 0))
    halo_buf = pltpu.VMEM((HALO + tm, D_MODEL), F32)
    return pl.pallas_call(
        _pool_kernel,
        grid=(batch, seq // tm),
        in_specs=[tile,
                  pl.BlockSpec((None, 6, D_MODEL), lambda b_, t: (b_, 0, 0)),
                  _const_spec((4, D_MODEL)),
                  _const_spec(w.shape), _const_spec(b.shape), _const_spec(scale.shape),
                  _const_spec(w1.shape), _const_spec(w2.shape)],
        out_specs=tile,
        out_shape=jax.ShapeDtypeStruct(x.shape, F32),
        scratch_shapes=[halo_buf, halo_buf, halo_buf, pltpu.VMEM((tm, D_MODEL), F32)],
        compiler_params=_params("arbitrary", "arbitrary"),
        name="pool_ffn",
    )(x, mod, ng, w, b, scale, w1, w2)


def _pad_heads(w, per_head, lane0=0):
    k = w.shape[0]
    w = w.reshape(k, MLA_HEADS, per_head)
    w = jnp.pad(w, ((0, 0), (0, 0), (lane0, HEAD_PAD - lane0 - per_head)))
    return w.reshape(k, MLA_HEADS * HEAD_PAD)


def _swap_halves(w):
    half = w.shape[-1] // 2
    return jnp.concatenate([w[..., half:], w[..., :half]], axis=-1)


def _mla_weights(w_dq, q_norm_g, w_uq, w_dkv, kv_norm_g, w_ukv, w_o):
    uq = w_uq.reshape(Q_LORA, MLA_HEADS, QK_NOPE + QK_ROPE)
    uq_nope = uq[..., :QK_NOPE].reshape(Q_LORA, -1)
    uq_rope = uq[..., QK_NOPE:]
    rope_cols = lambda w3: _pad_heads(w3.reshape(Q_LORA, -1), QK_ROPE, ROPE_LANE0)
    w_uq_pad = _pad_heads(uq_nope, QK_NOPE) + rope_cols(uq_rope)
    w_uq_sw = rope_cols(_swap_halves(uq_rope))

    dkv_rope = w_dkv[:, KV_LORA:]
    pad_rope = lambda w: jnp.pad(w, ((0, 0), (ROPE_LANE0, LANES - ROPE_LANE0 - QK_ROPE)))
    w_dkv_all = jnp.concatenate(
        [w_dkv[:, :KV_LORA], pad_rope(dkv_rope), pad_rope(_swap_halves(dkv_rope))], axis=-1)

    ukv = w_ukv.reshape(KV_LORA, MLA_HEADS, QK_NOPE + V_HEAD)
    w_uk = _pad_heads(ukv[..., :QK_NOPE].reshape(KV_LORA, -1), QK_NOPE)
    w_uv = _pad_heads(ukv[..., QK_NOPE:].reshape(KV_LORA, -1), V_HEAD)

    return {
        "w_dq": w_dq.astype(BF16), "qg": q_norm_g.reshape(1, -1),
        "w_uq": w_uq_pad.astype(BF16), "w_uqs": w_uq_sw.astype(BF16),
        "w_dkv": w_dkv_all.astype(BF16), "kvg": kv_norm_g.reshape(1, -1),
        "w_uk": w_uk.astype(BF16), "w_uvt": w_uv.T.astype(BF16),
        "w_o": w_o.astype(BF16),
    }


def _rope_inv_freq_row():
    inv_freq = ROPE_THETA ** (-jnp.arange(0, QK_ROPE, 2, dtype=F32) / QK_ROPE)
    return jnp.tile(inv_freq, LANES // inv_freq.shape[0]).reshape(1, LANES)


def kernel(x, c, positions, ada_w, ada_b, norm_g, mla_w_dq, mla_q_norm_g, mla_w_uq, mla_w_dkv, mla_kv_norm_g, mla_w_ukv, mla_w_o, conv_w_pw1, conv_b_pw1, conv_w_dw, conv_b_dw, conv_ln_g, conv_ln_b, conv_w_pw2, conv_b_pw2, pool_w, pool_b, pool_scale, ffn_w1, ffn_w2):
    batch, seq, d = x.shape
    assert d == D_MODEL and seq % TOKEN_TILE == 0 and seq % ATTN_TILE == 0
    assert ATTN_TILE % CHUNK == 0 and TOKEN_TILE % CONV_ROWS == 0
    assert TOKEN_TILE == ATTN_TILE

    mod = _ada_mod(c, ada_w, ada_b)
    pos3 = positions.reshape(batch, seq, 1)
    invf = _rope_inv_freq_row()
    row = lambda v: v.reshape(1, -1)

    for i in range(DEPTH):
        kind = i % N_MIXERS
        j = i // N_MIXERS
        w1 = ffn_w1[i].astype(BF16)
        w2 = ffn_w2[i].astype(BF16)
        if kind == 0:
            w = _mla_weights(mla_w_dq[j], mla_q_norm_g[j], mla_w_uq[j], mla_w_dkv[j],
                             mla_kv_norm_g[j], mla_w_ukv[j], mla_w_o[j])
            q, k, vt = _mla_proj(x, pos3, mod[i], norm_g[i], invf, w)
            o = _attention(q, k, vt)
            x = _mla_out(x, o, mod[i], norm_g[i], w["w_o"], w1, w2)
        elif kind == 1:
            w = {
                "w_pw1": conv_w_pw1[j].astype(BF16), "b_pw1": row(conv_b_pw1[j]),
                "w_dw": conv_w_dw[j], "b_dw": row(conv_b_dw[j]),
                "ln_g": row(conv_ln_g[j]), "ln_b": row(conv_ln_b[j]),
                "w_pw2": conv_w_pw2[j].astype(BF16), "b_pw2": row(conv_b_pw2[j]),
            }
            x = _conv_layer(x, mod[i], norm_g[i], w, w1, w2)
        else:
            x = _pool_layer(x, mod[i], norm_g[i], pool_w[j].astype(BF16), pool_b[j],
                            row(pool_scale[j]), w1, w2)
    return x
```

```python
import functools
import math

import jax
import jax.numpy as jnp
from jax import lax
from jax.experimental import pallas as pl
from jax.experimental.pallas import tpu as pltpu

D_MODEL = 1024
DEPTH = 4
CHUNK = 64
N_MIXERS = 3
MLA_HEADS = 16
QK_NOPE = 64
QK_ROPE = 32
V_HEAD = 64
Q_LORA = 384
KV_LORA = 256
ROPE_THETA = 10000.0
CONV_WIDTH = 31
POOL_WINDOWS = (2, 4, 8, 16)
D_FF = 4 * D_MODEL
NORM_EPS = 1e-6
NEG_INF = -1e30

LANES = 128
HEAD_PAD = LANES
ROPE_LANE0 = QK_NOPE
MASK_LANE0 = QK_NOPE + QK_ROPE
ONES_ROW = V_HEAD
VMEM_LIMIT_BYTES = 56 * 1024 * 1024

TOKEN_TILE = 512
FF_CHUNK = 1024
ATTN_TILE = 512
ATTN_UNROLL = 6
ATTN_HEADS_PER_STEP = 2
SUBLANES = 8
CONV_ROWS = 64
CONV_COLS = 256
HALO = 32
ADA_COLS = 1536

F32 = jnp.float32
BF16 = jnp.bfloat16
_NT_DIMS = (((1,), (1,)), ((), ()))


def _const_spec(shape):
    zeros = (0,) * len(shape)
    return pl.BlockSpec(shape, lambda *_: zeros, pipeline_mode=pl.Buffered(1))


def _params(*semantics):
    return pltpu.CompilerParams(dimension_semantics=semantics,
                                vmem_limit_bytes=VMEM_LIMIT_BYTES)


def _dot(a, b):
    return jnp.dot(a, b, preferred_element_type=F32)


def _rms(x, g):
    return x * lax.rsqrt(jnp.mean(x * x, axis=-1, keepdims=True) + NORM_EPS) * g


def _ffn_sublayer(x, mod, ng, w1_ref, w2_ref):
    h = (_rms(x, ng[2:3]) * (1.0 + mod[4:5]) + mod[3:4]).astype(BF16)
    y = jnp.zeros(x.shape, F32)
    for c in range(D_FF // FF_CHUNK):
        cols = slice(c * FF_CHUNK, (c + 1) * FF_CHUNK)
        a = jnp.maximum(_dot(h, w1_ref[:, cols]), 0.0)
        y = y + _dot((a * a).astype(BF16), w2_ref[cols, :])
    return x + mod[5:6] * _rms(y, ng[3:4])


def _ada_kernel(c_ref, w_ref, b_ref, o_ref):
    c = c_ref[...]
    c_act = (c * jax.nn.sigmoid(c)).astype(BF16)
    o_ref[...] = _dot(c_act, w_ref[...].astype(BF16)) + b_ref[...]


def _ada_mod(c, ada_w, ada_b):
    batch = c.shape[0]
    n_cols = ada_w.shape[-1]
    out = pl.pallas_call(
        _ada_kernel,
        grid=(DEPTH, n_cols // ADA_COLS),
        in_specs=[
            pl.BlockSpec((batch, D_MODEL), lambda l, n: (0, 0)),
            pl.BlockSpec((None, D_MODEL, ADA_COLS), lambda l, n: (l, 0, n)),
            pl.BlockSpec((None, 1, ADA_COLS), lambda l, n: (l, 0, n)),
        ],
        out_specs=pl.BlockSpec((None, batch, ADA_COLS), lambda l, n: (l, 0, n)),
        out_shape=jax.ShapeDtypeStruct((DEPTH, batch, n_cols), F32),
        compiler_params=_params("arbitrary", "arbitrary"),
        name="ada_mod",
    )(c, ada_w, ada_b.reshape(DEPTH, 1, n_cols))
    return out.reshape(DEPTH, batch, 6, D_MODEL)


def _mla_proj_kernel(x_ref, pos_ref, mod_ref, ng_ref, invf_ref, w_dq_ref, qg_ref, w_uq_ref,
                     w_uqs_ref, w_dkv_ref, kvg_ref, w_uk_ref, w_uvt_ref,
                     q_ref, k_ref, vt_ref):
    mod = mod_ref[...]
    ng = ng_ref[...]
    h = (_rms(x_ref[...], ng[0:1]) * (1.0 + mod[1:2]) + mod[0:1]).astype(BF16)

    ang = pos_ref[...].astype(F32) * invf_ref[...]
    lane = lax.broadcasted_iota(jnp.int32, (1, LANES), 1)
    first_half = (lane >= ROPE_LANE0) & (lane < ROPE_LANE0 + QK_ROPE // 2)
    second_half = (lane >= ROPE_LANE0 + QK_ROPE // 2) & (lane < ROPE_LANE0 + QK_ROPE)
    sin = jnp.sin(ang)
    cos_t = jnp.where(first_half | second_half, jnp.cos(ang), 1.0)
    sin_t = jnp.where(first_half, -sin, jnp.where(second_half, sin, 0.0))

    cq = _rms(_dot(h, w_dq_ref[...]), qg_ref[...]).astype(BF16)
    ckv_all = _dot(h, w_dkv_ref[...])
    ckv = _rms(ckv_all[:, :KV_LORA], kvg_ref[...]).astype(BF16)
    k_rope = (ckv_all[:, KV_LORA:KV_LORA + LANES] * cos_t
              + ckv_all[:, KV_LORA + LANES:] * sin_t)
    tile_rows = ckv_all.shape[0]
    chunk_in_block = (lax.broadcasted_iota(jnp.int32, (tile_rows, LANES), 0) // CHUNK) % (ATTN_TILE // CHUNK)
    chunk_lane = lax.broadcasted_iota(jnp.int32, (tile_rows, LANES), 1) - MASK_LANE0
    k_rope = k_rope + jnp.where(chunk_lane == chunk_in_block, 1.0, 0.0)

    scale = math.log2(math.e) / math.sqrt(QK_NOPE + QK_ROPE)
    group = 4 * HEAD_PAD
    row_in_head = lax.broadcasted_iota(jnp.int32, (group, 1), 0) % HEAD_PAD
    ones_row = jnp.where(row_in_head == ONES_ROW, 1.0, 0.0)
    for g in range(MLA_HEADS * HEAD_PAD // group):
        cols = slice(g * group, (g + 1) * group)
        q = _dot(cq, w_uq_ref[:, cols])
        q_sw = _dot(cq, w_uqs_ref[:, cols])
        k = _dot(ckv, w_uk_ref[:, cols])
        for j in range(group // HEAD_PAD):
            sl = slice(j * HEAD_PAD, (j + 1) * HEAD_PAD)
            out = slice(g * group + j * HEAD_PAD, g * group + (j + 1) * HEAD_PAD)
            q_ref[:, out] = ((q[:, sl] * cos_t + q_sw[:, sl] * sin_t) * scale).astype(BF16)
            k_ref[:, out] = (k[:, sl] + k_rope).astype(BF16)
        vt = lax.dot_general(w_uvt_ref[cols, :], ckv, _NT_DIMS, preferred_element_type=F32)
        vt_ref[cols, :] = (vt + ones_row).astype(BF16)


def _mla_proj(x, pos3, mod, ng, invf, w):
    batch, seq, _ = x.shape
    tm = TOKEN_TILE
    wide = MLA_HEADS * HEAD_PAD
    tile = lambda width: pl.BlockSpec((None, tm, width), lambda b, t: (b, t, 0))
    out_sds = jax.ShapeDtypeStruct((batch, seq, wide), BF16)
    vt_sds = jax.ShapeDtypeStruct((batch, seq // tm, wide, tm), BF16)
    vt_spec = pl.BlockSpec((None, None, wide, tm), lambda b, t: (b, t, 0, 0))
    return pl.pallas_call(
        _mla_proj_kernel,
        grid=(batch, seq // tm),
        in_specs=[
            tile(D_MODEL),
            tile(1),
            pl.BlockSpec((None, 6, D_MODEL), lambda b, t: (b, 0, 0)),
            _const_spec((4, D_MODEL)),
            _const_spec((1, LANES)),
            _const_spec(w["w_dq"].shape),
            _const_spec(w["qg"].shape),
            _const_spec(w["w_uq"].shape),
            _const_spec(w["w_uqs"].shape),
            _const_spec(w["w_dkv"].shape),
            _const_spec(w["kvg"].shape),
            _const_spec(w["w_uk"].shape),
            _const_spec(w["w_uvt"].shape),
        ],
        out_specs=[tile(wide), tile(wide), vt_spec],
        out_shape=[out_sds, out_sds, vt_sds],
        compiler_params=_params("arbitrary", "arbitrary"),
        name="mla_proj",
    )(x, pos3, mod, ng, invf, w["w_dq"], w["qg"], w["w_uq"], w["w_uqs"], w["w_dkv"], w["kvg"],
      w["w_uk"], w["w_uvt"])


def _next_block(i, j):
    wrap = j >= i
    return jnp.where(wrap, i + 1, i), jnp.where(wrap, 0, j + 1)


def _attn_kernel(q_ref, k_ref, vt_ref, ot_ref, m_sc, acc_sc, feat_sc, *stage_sc):
    t = ATTN_TILE
    heads = [slice(hh * HEAD_PAD, (hh + 1) * HEAD_PAD) for hh in range(ATTN_HEADS_PER_STEP)]
    n_tiles = q_ref.shape[0] // t
    n_blocks = n_tiles * (n_tiles + 1) // 2
    n_heads = len(heads)
    per_kind = 2 * n_heads
    by_slot = lambda refs: [refs[slot * n_heads:(slot + 1) * n_heads] for slot in range(2)]
    s_sc, smax_sc = (by_slot(stage_sc[kind * per_kind:(kind + 1) * per_kind]) for kind in range(2))

    row_chunk = lax.broadcasted_iota(jnp.int32, (t, HEAD_PAD), 0) // CHUNK
    later_chunk = lax.broadcasted_iota(jnp.int32, (t, HEAD_PAD), 1) - MASK_LANE0
    masked_lane = (later_chunk > row_chunk) & (later_chunk < t // CHUNK)
    feat_sc[0] = jnp.zeros((t, HEAD_PAD), BF16)
    feat_sc[1] = jnp.where(masked_lane, NEG_INF, 0.0).astype(BF16)
    acc_sc[...] = jnp.zeros(acc_sc.shape, F32)

    def rows(idx):
        return pl.ds(pl.multiple_of(idx * t, t), t)

    def scores(blk, slot):
        i, j = blk
        feat = feat_sc[(i == j).astype(jnp.int32)]
        for hh, head in enumerate(heads):
            s = lax.dot_general(k_ref[rows(j), head], q_ref[rows(i), head] + feat,
                                _NT_DIMS, preferred_element_type=F32)
            s_sc[slot][hh][...] = s
            smax_sc[slot][hh][...] = jnp.max(s, axis=0, keepdims=True)

    def softmax_values(blk, slot):
        i, j = blk
        for hh, head in enumerate(heads):
            m_old = jnp.where(j == 0, NEG_INF, m_sc[hh:hh + 1, :])
            m_new = jnp.maximum(m_old, smax_sc[slot][hh][...])
            p = jnp.exp2(s_sc[slot][hh][...] - m_new).astype(BF16)
            m_sc[hh:hh + 1, :] = m_new
            acc = jnp.exp2(m_old - m_new) * acc_sc[head, :] + _dot(vt_ref[j, head, :], p)
            acc_sc[head, :] = acc
            inv_sum = 1.0 / acc[ONES_ROW:ONES_ROW + 1, :]
            ot_ref[i, hh * V_HEAD:(hh + 1) * V_HEAD, :] = (acc[:V_HEAD, :] * inv_sum).astype(BF16)

    def run_blocks(blk, prefetch_last):
        for u in range(ATTN_UNROLL):
            nxt = _next_block(*blk)
            if u + 1 < ATTN_UNROLL or prefetch_last:
                scores(nxt, (u + 1) % 2)
            softmax_values(blk, u % 2)
            blk = nxt
        return blk

    zero = jnp.int32(0)
    scores((zero, zero), 0)
    assert ATTN_UNROLL % 2 == 0 and n_blocks % ATTN_UNROLL == 0
    blk = lax.fori_loop(0, n_blocks // ATTN_UNROLL - 1, lambda _, b: run_blocks(b, True), (zero, zero))
    run_blocks(blk, False)


def _attention(q, k, vt):
    batch, seq, _ = q.shape
    t = ATTN_TILE
    n = ATTN_HEADS_PER_STEP
    width = n * HEAD_PAD
    seq_spec = pl.BlockSpec((None, seq, width), lambda b, h: (b, 0, h))
    transposed_spec = pl.BlockSpec((None, seq // t, width, t), lambda b, h: (b, 0, h, 0))
    return pl.pallas_call(
        _attn_kernel,
        grid=(batch, MLA_HEADS // n),
        in_specs=[seq_spec, seq_spec, transposed_spec],
        out_specs=pl.BlockSpec((None, seq // t, n * V_HEAD, t), lambda b, h: (b, 0, h, 0)),
        out_shape=jax.ShapeDtypeStruct((batch, seq // t, MLA_HEADS * V_HEAD, t), BF16),
        scratch_shapes=[
            pltpu.VMEM((n, t), F32),
            pltpu.VMEM((width, t), F32),
            pltpu.VMEM((2, t, HEAD_PAD), BF16),
        ]
        + [pltpu.VMEM((t, t), F32)] * (2 * n) + [pltpu.VMEM((1, t), F32)] * (2 * n),
        compiler_params=_params("arbitrary", "arbitrary"),
        name="mla_attention",
    )(q, k, vt)


def _mla_out_kernel(x_ref, ot_ref, mod_ref, ng_ref, w_o_ref, w1_ref, w2_ref, out_ref):
    mod = mod_ref[...]
    ng = ng_ref[...]
    y = lax.dot_general(ot_ref[...], w_o_ref[...], (((0,), (0,)), ((), ())),
                        preferred_element_type=F32)
    x = x_ref[...] + mod[2:3] * _rms(y, ng[1:2])
    out_ref[...] = _ffn_sublayer(x, mod, ng, w1_ref, w2_ref)


def _mla_out(x, o, mod, ng, w_o, w1, w2):
    batch, seq, _ = x.shape
    tm = TOKEN_TILE
    tile = lambda width: pl.BlockSpec((None, tm, width), lambda b, t: (b, t, 0))
    return pl.pallas_call(
        _mla_out_kernel,
        grid=(batch, seq // tm),
        in_specs=[
            tile(D_MODEL),
            pl.BlockSpec((None, None, o.shape[2], tm), lambda b, t: (b, t, 0, 0)),
            pl.BlockSpec((None, 6, D_MODEL), lambda b, t: (b, 0, 0)),
            _const_spec((4, D_MODEL)),
            _const_spec(w_o.shape),
            _const_spec(w1.shape),
            _const_spec(w2.shape),
        ],
        out_specs=tile(D_MODEL),
        out_shape=jax.ShapeDtypeStruct(x.shape, F32),
        compiler_params=_params("arbitrary", "arbitrary"),
        name="mla_out_ffn",
    )(x, o, mod, ng, w_o, w1, w2)


def _conv_kernel(x_ref, mod_ref, ng_ref, w_pw1_ref, b_pw1_ref, w_dw_ref, b_dw_ref, ln_g_ref,
                 ln_b_ref, w_pw2_ref, b_pw2_ref, w1_ref, w2_ref, out_ref, u_sc, v_sc, c_sc):
    tm = TOKEN_TILE
    mod = mod_ref[...]
    ng = ng_ref[...]
    x = x_ref[...]
    h = (_rms(x, ng[0:1]) * (1.0 + mod[1:2]) + mod[0:1]).astype(BF16)
    a = _dot(h, w_pw1_ref[...]) + b_pw1_ref[...]
    u = a[:, :D_MODEL] * jax.nn.sigmoid(a[:, D_MODEL:])

    @pl.when(pl.program_id(1) == 0)
    def _():
        u_sc[0:HALO, :] = jnp.zeros((HALO, D_MODEL), F32)

    @pl.when(pl.program_id(1) != 0)
    def _():
        u_sc[0:HALO, :] = u_sc[tm:tm + HALO, :]

    u_sc[HALO:HALO + tm, :] = u

    first_tap = HALO - (CONV_WIDTH - 1)
    taps = [[] for _ in range(SUBLANES)]
    for j in range(CONV_WIDTH):
        taps[(first_tap + j) % SUBLANES].append(((first_tap + j) // SUBLANES, j))
    w_dw = w_dw_ref[...]
    b_dw = b_dw_ref[...]

    for c0 in range(0, D_MODEL, CONV_COLS):
        cols = slice(c0, c0 + CONV_COLS)

        def partial_sum(r, r0, nrows):
            return sum(u_sc[r0 + SUBLANES * a:r0 + SUBLANES * a + nrows, cols] * w_dw[j:j + 1, cols]
                       for a, j in taps[r])

        for r0 in range(0, tm + SUBLANES, CONV_ROWS):
            nrows = min(CONV_ROWS, tm + SUBLANES - r0)
            for r in range(1, SUBLANES):
                v_sc[r - 1, r0:r0 + nrows, :] = partial_sum(r, r0, nrows)
        for r0 in range(0, tm, CONV_ROWS):
            acc = b_dw[:, cols] + partial_sum(0, r0, CONV_ROWS)
            for r in range(1, SUBLANES):
                acc = acc + v_sc[r - 1, r0 + r:r0 + r + CONV_ROWS, :]
            c_sc[r0:r0 + CONV_ROWS, cols] = acc

    cv = c_sc[...]
    mu = jnp.mean(cv, axis=-1, keepdims=True)
    xc = cv - mu
    ln = xc * lax.rsqrt(jnp.mean(xc * xc, axis=-1, keepdims=True) + NORM_EPS)
    ln = ln * ln_g_ref[...] + ln_b_ref[...]
    act = (ln * jax.nn.sigmoid(ln)).astype(BF16)
    y = _dot(act, w_pw2_ref[...]) + b_pw2_ref[...]
    x = x + mod[2:3] * _rms(y, ng[1:2])
    out_ref[...] = _ffn_sublayer(x, mod, ng, w1_ref, w2_ref)


def _conv_layer(x, mod, ng, w, w1, w2):
    batch, seq, _ = x.shape
    tm = TOKEN_TILE
    tile = pl.BlockSpec((None, tm, D_MODEL), lambda b, t: (b, t, 0))
    names = ("w_pw1", "b_pw1", "w_dw", "b_dw", "ln_g", "ln_b", "w_pw2", "b_pw2")
    return pl.pallas_call(
        _conv_kernel,
        grid=(batch, seq // tm),
        in_specs=[tile,
                  pl.BlockSpec((None, 6, D_MODEL), lambda b, t: (b, 0, 0)),
                  _const_spec((4, D_MODEL))]
                 + [_const_spec(w[n].shape) for n in names]
                 + [_const_spec(w1.shape), _const_spec(w2.shape)],
        out_specs=tile,
        out_shape=jax.ShapeDtypeStruct(x.shape, F32),
        scratch_shapes=[pltpu.VMEM((HALO + tm, D_MODEL), F32),
                        pltpu.VMEM((SUBLANES - 1, tm + SUBLANES, CONV_COLS), F32),
                        pltpu.VMEM((tm, D_MODEL), F32)],
        compiler_params=_params("arbitrary", "arbitrary"),
        name="conv_ffn",
    )(x, mod, ng, *[w[n] for n in names], w1, w2)


def _pool_kernel(x_ref, mod_ref, ng_ref, w_ref, b_ref, scale_ref, w1_ref, w2_ref, out_ref,
                 h_sc, s_a, s_b, y_sc):
    tm = TOKEN_TILE
    n_groups = len(POOL_WINDOWS)
    width = D_MODEL // n_groups
    mod = mod_ref[...]
    ng = ng_ref[...]
    x = x_ref[...]
    h = _rms(x, ng[0:1]) * (1.0 + mod[1:2]) + mod[0:1]

    @pl.when(pl.program_id(1) == 0)
    def _():
        h_sc[0:HALO, :] = jnp.zeros((HALO, D_MODEL), F32)

    @pl.when(pl.program_id(1) != 0)
    def _():
        h_sc[0:HALO, :] = h_sc[tm:tm + HALO, :]

    h_sc[HALO:HALO + tm, :] = h

    n = HALO + tm
    s_a[8:n, :] = h_sc[8:n, :] + h_sc[7:n - 1, :]
    s_b[16:n, width:] = s_a[16:n, width:] + s_a[14:n - 2, width:]
    s_a[24:n, 2 * width:] = s_b[24:n, 2 * width:] + s_b[20:n - 4, 2 * width:]
    s_b[32:n, 3 * width:] = s_a[32:n, 3 * width:] + s_a[24:n - 8, 3 * width:]

    t_idx = pl.program_id(1) * tm + lax.broadcasted_iota(jnp.int32, (tm, 1), 0)
    sums = (s_a, s_b, s_a, s_b)
    for g, win in enumerate(POOL_WINDOWS):
        cols = slice(g * width, (g + 1) * width)
        cnt = jnp.minimum(t_idx + 1, win).astype(F32)
        p = sums[g][HALO:n, cols] / cnt - h[:, cols]
        y_sc[:, cols] = _dot(p.astype(BF16), w_ref[g]) + b_ref[g:g + 1, :]

    y = y_sc[...] * scale_ref[...]
    x = x + mod[2:3] * _rms(y, ng[1:2])
    out_ref[...] = _ffn_sublayer(x, mod, ng, w1_ref, w2_ref)


def _pool_layer(x, mod, ng, w, b, scale, w1, w2):
    batch, seq, _ = x.shape
    tm = TOKEN_TILE
    tile = pl.BlockSpec((None, tm, D_MODEL), lambda b_, t: (b_, t, 0))
    halo_buf = pltpu.VMEM((HALO + tm, D_MODEL), F32)
    return pl.pallas_call(
        _pool_kernel,
        grid=(batch, seq // tm),
        in_specs=[tile,
                  pl.BlockSpec((None, 6, D_MODEL), lambda b_, t: (b_, 0, 0)),
                  _const_spec((4, D_MODEL)),
                  _const_spec(w.shape), _const_spec(b.shape), _const_spec(scale.shape),
                  _const_spec(w1.shape), _const_spec(w2.shape)],
        out_specs=tile,
        out_shape=jax.ShapeDtypeStruct(x.shape, F32),
        scratch_shapes=[halo_buf, halo_buf, halo_buf, pltpu.VMEM((tm, D_MODEL), F32)],
        compiler_params=_params("arbitrary", "arbitrary"),
        name="pool_ffn",
    )(x, mod, ng, w, b, scale, w1, w2)


def _pad_heads(w, per_head, lane0=0):
    k = w.shape[0]
    w = w.reshape(k, MLA_HEADS, per_head)
    w = jnp.pad(w, ((0, 0), (0, 0), (lane0, HEAD_PAD - lane0 - per_head)))
    return w.reshape(k, MLA_HEADS * HEAD_PAD)


def _swap_halves(w):
    half = w.shape[-1] // 2
    return jnp.concatenate([w[..., half:], w[..., :half]], axis=-1)


def _mla_weights(w_dq, q_norm_g, w_uq, w_dkv, kv_norm_g, w_ukv, w_o):
    uq = w_uq.reshape(Q_LORA, MLA_HEADS, QK_NOPE + QK_ROPE)
    uq_nope = uq[..., :QK_NOPE].reshape(Q_LORA, -1)
    uq_rope = uq[..., QK_NOPE:]
    rope_cols = lambda w3: _pad_heads(w3.reshape(Q_LORA, -1), QK_ROPE, ROPE_LANE0)
    w_uq_pad = _pad_heads(uq_nope, QK_NOPE) + rope_cols(uq_rope)
    w_uq_sw = rope_cols(_swap_halves(uq_rope))

    dkv_rope = w_dkv[:, KV_LORA:]
    pad_rope = lambda w: jnp.pad(w, ((0, 0), (ROPE_LANE0, LANES - ROPE_LANE0 - QK_ROPE)))
    w_dkv_all = jnp.concatenate(
        [w_dkv[:, :KV_LORA], pad_rope(dkv_rope), pad_rope(_swap_halves(dkv_rope))], axis=-1)

    ukv = w_ukv.reshape(KV_LORA, MLA_HEADS, QK_NOPE + V_HEAD)
    w_uk = _pad_heads(ukv[..., :QK_NOPE].reshape(KV_LORA, -1), QK_NOPE)
    w_uv = _pad_heads(ukv[..., QK_NOPE:].reshape(KV_LORA, -1), V_HEAD)

    return {
        "w_dq": w_dq.astype(BF16), "qg": q_norm_g.reshape(1, -1),
        "w_uq": w_uq_pad.astype(BF16), "w_uqs": w_uq_sw.astype(BF16),
        "w_dkv": w_dkv_all.astype(BF16), "kvg": kv_norm_g.reshape(1, -1),
        "w_uk": w_uk.astype(BF16), "w_uvt": w_uv.T.astype(BF16),
        "w_o": w_o.astype(BF16),
    }


def _rope_inv_freq_row():
    inv_freq = ROPE_THETA ** (-jnp.arange(0, QK_ROPE, 2, dtype=F32) / QK_ROPE)
    return jnp.tile(inv_freq, LANES // inv_freq.shape[0]).reshape(1, LANES)


def kernel(x, c, positions, ada_w, ada_b, norm_g, mla_w_dq, mla_q_norm_g, mla_w_uq, mla_w_dkv, mla_kv_norm_g, mla_w_ukv, mla_w_o, conv_w_pw1, conv_b_pw1, conv_w_dw, conv_b_dw, conv_ln_g, conv_ln_b, conv_w_pw2, conv_b_pw2, pool_w, pool_b, pool_scale, ffn_w1, ffn_w2):
    batch, seq, d = x.shape
    assert d == D_MODEL and seq % TOKEN_TILE == 0 and seq % ATTN_TILE == 0
    assert ATTN_TILE % CHUNK == 0 and TOKEN_TILE % CONV_ROWS == 0
    assert TOKEN_TILE == ATTN_TILE

    mod = _ada_mod(c, ada_w, ada_b)
    pos3 = positions.reshape(batch, seq, 1)
    invf = _rope_inv_freq_row()
    row = lambda v: v.reshape(1, -1)

    for i in range(DEPTH):
        kind = i % N_MIXERS
        j = i // N_MIXERS
        w1 = ffn_w1[i].astype(BF16)
        w2 = ffn_w2[i].astype(BF16)
        if kind == 0:
            w = _mla_weights(mla_w_dq[j], mla_q_norm_g[j], mla_w_uq[j], mla_w_dkv[j],
                             mla_kv_norm_g[j], mla_w_ukv[j], mla_w_o[j])
            q, k, vt = _mla_proj(x, pos3, mod[i], norm_g[i], invf, w)
            o = _attention(q, k, vt)
            x = _mla_out(x, o, mod[i], norm_g[i], w["w_o"], w1, w2)
        elif kind == 1:
            w = {
                "w_pw1": conv_w_pw1[j].astype(BF16), "b_pw1": row(conv_b_pw1[j]),
                "w_dw": conv_w_dw[j], "b_dw": row(conv_b_dw[j]),
                "ln_g": row(conv_ln_g[j]), "ln_b": row(conv_ln_b[j]),
                "w_pw2": conv_w_pw2[j].astype(BF16), "b_pw2": row(conv_b_pw2[j]),
            }
            x = _conv_layer(x, mod[i], norm_g[i], w, w1, w2)
        else:
            x = _pool_layer(x, mod[i], norm_g[i], pool_w[j].astype(BF16), pool_b[j],
                            row(pool_scale[j]), w1, w2)
    return x
```

```python
import functools
import math

import jax
import jax.numpy as jnp
from jax import lax
from jax.experimental import pallas as pl
from jax.experimental.pallas import tpu as pltpu

D_MODEL = 1024
DEPTH = 4
CHUNK = 64
N_MIXERS = 3
MLA_HEADS = 16
QK_NOPE = 64
QK_ROPE = 32
V_HEAD = 64
Q_LORA = 384
KV_LORA = 256
ROPE_THETA = 10000.0
CONV_WIDTH = 31
POOL_WINDOWS = (2, 4, 8, 16)
D_FF = 4 * D_MODEL
NORM_EPS = 1e-6
NEG_INF = -1e30

LANES = 128
HEAD_PAD = LANES
ROPE_LANE0 = QK_NOPE
MASK_LANE0 = QK_NOPE + QK_ROPE
ONES_ROW = V_HEAD
VMEM_LIMIT_BYTES = 56 * 1024 * 1024

TOKEN_TILE = 512
FF_CHUNK = 1024
ATTN_TILE = 512
ATTN_UNROLL = 6
ATTN_HEADS_PER_STEP = 2
SUBLANES = 8
CONV_ROWS = 64
CONV_COLS = 256
HALO = 32
ADA_COLS = 1536

F32 = jnp.float32
BF16 = jnp.bfloat16
_NT_DIMS = (((1,), (1,)), ((), ()))


def _const_spec(shape):
    zeros = (0,) * len(shape)
    return pl.BlockSpec(shape, lambda *_: zeros, pipeline_mode=pl.Buffered(1))


def _layer_spec(stacked, layer):
    index = (layer,) + (0,) * (stacked.ndim - 1)
    return pl.BlockSpec((None,) + stacked.shape[1:], lambda *_: index, pipeline_mode=pl.Buffered(1))


def _params(*semantics):
    return pltpu.CompilerParams(dimension_semantics=semantics,
                                vmem_limit_bytes=VMEM_LIMIT_BYTES)


def _dot(a, b):
    return jnp.dot(a, b, preferred_element_type=F32)


def _rms(x, g):
    return x * lax.rsqrt(jnp.mean(x * x, axis=-1, keepdims=True) + NORM_EPS) * g


def _ffn_sublayer(x, mod, ng, w1_ref, w2_ref):
    h = (_rms(x, ng[2:3]) * (1.0 + mod[4:5]) + mod[3:4]).astype(BF16)
    y = jnp.zeros(x.shape, F32)
    for c in range(D_FF // FF_CHUNK):
        cols = slice(c * FF_CHUNK, (c + 1) * FF_CHUNK)
        a = jnp.maximum(_dot(h, w1_ref[:, cols]), 0.0)
        y = y + _dot((a * a).astype(BF16), w2_ref[cols, :])
    return x + mod[5:6] * _rms(y, ng[3:4])


def _ada_kernel(c_ref, w_ref, b_ref, o_ref):
    c = c_ref[...]
    c_act = (c * jax.nn.sigmoid(c)).astype(BF16)
    o_ref[...] = _dot(c_act, w_ref[...].astype(BF16)) + b_ref[...]


def _ada_mod(c, ada_w, ada_b):
    batch = c.shape[0]
    n_cols = ada_w.shape[-1]
    out = pl.pallas_call(
        _ada_kernel,
        grid=(DEPTH, n_cols // ADA_COLS),
        in_specs=[
            pl.BlockSpec((batch, D_MODEL), lambda l, n: (0, 0)),
            pl.BlockSpec((None, D_MODEL, ADA_COLS), lambda l, n: (l, 0, n)),
            pl.BlockSpec((None, 1, ADA_COLS), lambda l, n: (l, 0, n)),
        ],
        out_specs=pl.BlockSpec((None, batch, ADA_COLS), lambda l, n: (l, 0, n)),
        out_shape=jax.ShapeDtypeStruct((DEPTH, batch, n_cols), F32),
        compiler_params=_params("arbitrary", "arbitrary"),
        name="ada_mod",
    )(c, ada_w, ada_b.reshape(DEPTH, 1, n_cols))
    return out.reshape(DEPTH, batch, 6, D_MODEL)


def _mla_proj_kernel(x_ref, pos_ref, mod_ref, ng_ref, invf_ref, w_dq_ref, qg_ref, w_uq_ref,
                     w_dkv_ref, kvg_ref, w_uk_ref, w_uvt_ref,
                     q_ref, k_ref, vt_ref):
    mod = mod_ref[...]
    ng = ng_ref[...]
    h = (_rms(x_ref[...], ng[0:1]) * (1.0 + mod[1:2]) + mod[0:1]).astype(BF16)

    ang = pos_ref[...].astype(F32) * invf_ref[...]
    lane = lax.broadcasted_iota(jnp.int32, (1, LANES), 1)
    first_half = (lane >= ROPE_LANE0) & (lane < ROPE_LANE0 + QK_ROPE // 2)
    second_half = (lane >= ROPE_LANE0 + QK_ROPE // 2) & (lane < ROPE_LANE0 + QK_ROPE)
    sin = jnp.sin(ang)
    cos_t = jnp.where(first_half | second_half, jnp.cos(ang), 1.0)
    sin_t = jnp.where(first_half, -sin, jnp.where(second_half, sin, 0.0))

    cq = _rms(_dot(h, w_dq_ref[...]), qg_ref[...]).astype(BF16)
    ckv_all = _dot(h, w_dkv_ref[...])
    ckv = _rms(ckv_all[:, :KV_LORA], kvg_ref[...]).astype(BF16)
    k_rope = (ckv_all[:, KV_LORA:KV_LORA + LANES] * cos_t
              + ckv_all[:, KV_LORA + LANES:] * sin_t)
    tile_rows = ckv_all.shape[0]
    chunk_in_block = (lax.broadcasted_iota(jnp.int32, (tile_rows, LANES), 0) // CHUNK) % (ATTN_TILE // CHUNK)
    chunk_lane = lax.broadcasted_iota(jnp.int32, (tile_rows, LANES), 1) - MASK_LANE0
    k_rope = k_rope + jnp.where(chunk_lane == chunk_in_block, 1.0, 0.0)

    scale = math.log2(math.e) / math.sqrt(QK_NOPE + QK_ROPE)
    group = 4 * HEAD_PAD
    row_in_head = lax.broadcasted_iota(jnp.int32, (group, 1), 0) % HEAD_PAD
    ones_row = jnp.where(row_in_head == ONES_ROW, 1.0, 0.0)
    for g in range(MLA_HEADS * HEAD_PAD // group):
        cols = slice(g * group, (g + 1) * group)
        q = _dot(cq, w_uq_ref[:, cols])
        k = _dot(ckv, w_uk_ref[:, cols])
        for j in range(group // HEAD_PAD):
            sl = slice(j * HEAD_PAD, (j + 1) * HEAD_PAD)
            out = slice(g * group + j * HEAD_PAD, g * group + (j + 1) * HEAD_PAD)
            qh = q[:, sl]
            half = QK_ROPE // 2
            q_swapped = jnp.where(first_half, pltpu.roll(qh, LANES - half, 1), pltpu.roll(qh, half, 1))
            q_ref[:, out] = ((qh * cos_t + q_swapped * sin_t) * scale).astype(BF16)
            k_ref[:, out] = (k[:, sl] + k_rope).astype(BF16)
        vt = lax.dot_general(w_uvt_ref[cols, :], ckv, _NT_DIMS, preferred_element_type=F32)
        vt_ref[cols, :] = (vt + ones_row).astype(BF16)


def _mla_proj(x, pos3, mod, ng, invf, w):
    batch, seq, _ = x.shape
    tm = TOKEN_TILE
    wide = MLA_HEADS * HEAD_PAD
    tile = lambda width: pl.BlockSpec((None, tm, width), lambda b, t: (b, t, 0))
    out_sds = jax.ShapeDtypeStruct((batch, seq, wide), BF16)
    vt_sds = jax.ShapeDtypeStruct((batch, seq // tm, wide, tm), BF16)
    vt_spec = pl.BlockSpec((None, None, wide, tm), lambda b, t: (b, t, 0, 0))
    return pl.pallas_call(
        _mla_proj_kernel,
        grid=(batch, seq // tm),
        in_specs=[
            tile(D_MODEL),
            tile(1),
            pl.BlockSpec((None, 6, D_MODEL), lambda b, t: (b, 0, 0)),
            _const_spec((4, D_MODEL)),
            _const_spec((1, LANES)),
            _const_spec(w["w_dq"].shape),
            _const_spec(w["qg"].shape),
            _const_spec(w["w_uq"].shape),
            _const_spec(w["w_dkv"].shape),
            _const_spec(w["kvg"].shape),
            _const_spec(w["w_uk"].shape),
            _const_spec(w["w_uvt"].shape),
        ],
        out_specs=[tile(wide), tile(wide), vt_spec],
        out_shape=[out_sds, out_sds, vt_sds],
        compiler_params=_params("arbitrary", "arbitrary"),
        name="mla_proj",
    )(x, pos3, mod, ng, invf, w["w_dq"], w["qg"], w["w_uq"], w["w_dkv"], w["kvg"],
      w["w_uk"], w["w_uvt"])


def _next_block(i, j):
    wrap = j >= i
    return jnp.where(wrap, i + 1, i), jnp.where(wrap, 0, j + 1)


def _attn_kernel(q_ref, k_ref, vt_ref, ot_ref, m_sc, acc_sc, feat_sc, *stage_sc):
    t = ATTN_TILE
    heads = [slice(hh * HEAD_PAD, (hh + 1) * HEAD_PAD) for hh in range(ATTN_HEADS_PER_STEP)]
    n_tiles = q_ref.shape[0] // t
    n_blocks = n_tiles * (n_tiles + 1) // 2
    n_heads = len(heads)
    per_kind = 2 * n_heads
    by_slot = lambda refs: [refs[slot * n_heads:(slot + 1) * n_heads] for slot in range(2)]
    s_sc, smax_sc = (by_slot(stage_sc[kind * per_kind:(kind + 1) * per_kind]) for kind in range(2))

    row_chunk = lax.broadcasted_iota(jnp.int32, (t, HEAD_PAD), 0) // CHUNK
    later_chunk = lax.broadcasted_iota(jnp.int32, (t, HEAD_PAD), 1) - MASK_LANE0
    masked_lane = (later_chunk > row_chunk) & (later_chunk < t // CHUNK)
    feat_sc[0] = jnp.zeros((t, HEAD_PAD), BF16)
    feat_sc[1] = jnp.where(masked_lane, NEG_INF, 0.0).astype(BF16)
    acc_sc[...] = jnp.zeros(acc_sc.shape, F32)

    def rows(idx):
        return pl.ds(pl.multiple_of(idx * t, t), t)

    def scores(blk, slot):
        i, j = blk
        feat = feat_sc[(i == j).astype(jnp.int32)]
        for hh, head in enumerate(heads):
            s = lax.dot_general(k_ref[rows(j), head], q_ref[rows(i), head] + feat,
                                _NT_DIMS, preferred_element_type=F32)
            s_sc[slot][hh][...] = s
            smax_sc[slot][hh][...] = jnp.max(s, axis=0, keepdims=True)

    def softmax_values(blk, slot):
        i, j = blk
        for hh, head in enumerate(heads):
            m_old = jnp.where(j == 0, NEG_INF, m_sc[hh:hh + 1, :])
            m_new = jnp.maximum(m_old, smax_sc[slot][hh][...])
            p = jnp.exp2(s_sc[slot][hh][...] - m_new).astype(BF16)
            m_sc[hh:hh + 1, :] = m_new
            acc = jnp.exp2(m_old - m_new) * acc_sc[head, :] + _dot(vt_ref[j, head, :], p)
            acc_sc[head, :] = acc
            inv_sum = 1.0 / acc[ONES_ROW:ONES_ROW + 1, :]
            ot_ref[i, hh * V_HEAD:(hh + 1) * V_HEAD, :] = (acc[:V_HEAD, :] * inv_sum).astype(BF16)

    def run_blocks(blk, prefetch_last):
        for u in range(ATTN_UNROLL):
            nxt = _next_block(*blk)
            if u + 1 < ATTN_UNROLL or prefetch_last:
                scores(nxt, (u + 1) % 2)
            softmax_values(blk, u % 2)
            blk = nxt
        return blk

    zero = jnp.int32(0)
    scores((zero, zero), 0)
    assert ATTN_UNROLL % 2 == 0 and n_blocks % ATTN_UNROLL == 0
    blk = lax.fori_loop(0, n_blocks // ATTN_UNROLL - 1, lambda _, b: run_blocks(b, True), (zero, zero))
    run_blocks(blk, False)


def _attention(q, k, vt):
    batch, seq, _ = q.shape
    t = ATTN_TILE
    n = ATTN_HEADS_PER_STEP
    width = n * HEAD_PAD
    seq_spec = pl.BlockSpec((None, seq, width), lambda b, h: (b, 0, h))
    transposed_spec = pl.BlockSpec((None, seq // t, width, t), lambda b, h: (b, 0, h, 0))
    return pl.pallas_call(
        _attn_kernel,
        grid=(batch, MLA_HEADS // n),
        in_specs=[seq_spec, seq_spec, transposed_spec],
        out_specs=pl.BlockSpec((None, seq // t, n * V_HEAD, t), lambda b, h: (b, 0, h, 0)),
        out_shape=jax.ShapeDtypeStruct((batch, seq // t, MLA_HEADS * V_HEAD, t), BF16),
        scratch_shapes=[
            pltpu.VMEM((n, t), F32),
            pltpu.VMEM((width, t), F32),
            pltpu.VMEM((2, t, HEAD_PAD), BF16),
        ]
        + [pltpu.VMEM((t, t), F32)] * (2 * n) + [pltpu.VMEM((1, t), F32)] * (2 * n),
        compiler_params=_params("arbitrary", "arbitrary"),
        name="mla_attention",
    )(q, k, vt)


def _mla_out_kernel(x_ref, ot_ref, mod_ref, ng_ref, w_o_ref, w1_ref, w2_ref, out_ref):
    mod = mod_ref[...]
    ng = ng_ref[...]
    y = lax.dot_general(ot_ref[...], w_o_ref[...], (((0,), (0,)), ((), ())),
                        preferred_element_type=F32)
    x = x_ref[...] + mod[2:3] * _rms(y, ng[1:2])
    out_ref[...] = _ffn_sublayer(x, mod, ng, w1_ref, w2_ref)


def _mla_out(x, o, mod, ng, w_o, w1, w2, layer):
    batch, seq, _ = x.shape
    tm = TOKEN_TILE
    tile = lambda width: pl.BlockSpec((None, tm, width), lambda b, t: (b, t, 0))
    return pl.pallas_call(
        _mla_out_kernel,
        grid=(batch, seq // tm),
        in_specs=[
            tile(D_MODEL),
            pl.BlockSpec((None, None, o.shape[2], tm), lambda b, t: (b, t, 0, 0)),
            pl.BlockSpec((None, 6, D_MODEL), lambda b, t: (b, 0, 0)),
            _const_spec((4, D_MODEL)),
            _const_spec(w_o.shape),
            _layer_spec(w1, layer),
            _layer_spec(w2, layer),
        ],
        out_specs=tile(D_MODEL),
        out_shape=jax.ShapeDtypeStruct(x.shape, F32),
        compiler_params=_params("arbitrary", "arbitrary"),
        name="mla_out_ffn",
    )(x, o, mod, ng, w_o, w1, w2)


def _conv_kernel(x_ref, mod_ref, ng_ref, w_pw1_ref, b_pw1_ref, w_dw_ref, b_dw_ref, ln_g_ref,
                 ln_b_ref, w_pw2_ref, b_pw2_ref, w1_ref, w2_ref, out_ref, u_sc, v_sc, c_sc):
    tm = TOKEN_TILE
    mod = mod_ref[...]
    ng = ng_ref[...]
    x = x_ref[...]
    h = (_rms(x, ng[0:1]) * (1.0 + mod[1:2]) + mod[0:1]).astype(BF16)
    a = _dot(h, w_pw1_ref[...]) + b_pw1_ref[...]
    u = a[:, :D_MODEL] * jax.nn.sigmoid(a[:, D_MODEL:])

    @pl.when(pl.program_id(1) == 0)
    def _():
        u_sc[0:HALO, :] = jnp.zeros((HALO, D_MODEL), F32)

    @pl.when(pl.program_id(1) != 0)
    def _():
        u_sc[0:HALO, :] = u_sc[tm:tm + HALO, :]

    u_sc[HALO:HALO + tm, :] = u

    first_tap = HALO - (CONV_WIDTH - 1)
    taps = [[] for _ in range(SUBLANES)]
    for j in range(CONV_WIDTH):
        taps[(first_tap + j) % SUBLANES].append(((first_tap + j) // SUBLANES, j))
    w_dw = w_dw_ref[...]
    b_dw = b_dw_ref[...]

    for c0 in range(0, D_MODEL, CONV_COLS):
        cols = slice(c0, c0 + CONV_COLS)

        def partial_sum(r, r0, nrows):
            return sum(u_sc[r0 + SUBLANES * a:r0 + SUBLANES * a + nrows, cols] * w_dw[j:j + 1, cols]
                       for a, j in taps[r])

        for r0 in range(0, tm + SUBLANES, CONV_ROWS):
            nrows = min(CONV_ROWS, tm + SUBLANES - r0)
            for r in range(1, SUBLANES):
                v_sc[r - 1, r0:r0 + nrows, :] = partial_sum(r, r0, nrows)
        for r0 in range(0, tm, CONV_ROWS):
            acc = b_dw[:, cols] + partial_sum(0, r0, CONV_ROWS)
            for r in range(1, SUBLANES):
                acc = acc + v_sc[r - 1, r0 + r:r0 + r + CONV_ROWS, :]
            c_sc[r0:r0 + CONV_ROWS, cols] = acc

    cv = c_sc[...]
    mu = jnp.mean(cv, axis=-1, keepdims=True)
    xc = cv - mu
    ln = xc * lax.rsqrt(jnp.mean(xc * xc, axis=-1, keepdims=True) + NORM_EPS)
    ln = ln * ln_g_ref[...] + ln_b_ref[...]
    act = (ln * jax.nn.sigmoid(ln)).astype(BF16)
    y = _dot(act, w_pw2_ref[...]) + b_pw2_ref[...]
    x = x + mod[2:3] * _rms(y, ng[1:2])
    out_ref[...] = _ffn_sublayer(x, mod, ng, w1_ref, w2_ref)


def _conv_layer(x, mod, ng, w, w1, w2, layer):
    batch, seq, _ = x.shape
    tm = TOKEN_TILE
    tile = pl.BlockSpec((None, tm, D_MODEL), lambda b, t: (b, t, 0))
    names = ("w_pw1", "b_pw1", "w_dw", "b_dw", "ln_g", "ln_b", "w_pw2", "b_pw2")
    return pl.pallas_call(
        _conv_kernel,
        grid=(batch, seq // tm),
        in_specs=[tile,
                  pl.BlockSpec((None, 6, D_MODEL), lambda b, t: (b, 0, 0)),
                  _const_spec((4, D_MODEL))]
                 + [_const_spec(w[n].shape) for n in names]
                 + [_layer_spec(w1, layer), _layer_spec(w2, layer)],
        out_specs=tile,
        out_shape=jax.ShapeDtypeStruct(x.shape, F32),
        scratch_shapes=[pltpu.VMEM((HALO + tm, D_MODEL), F32),
                        pltpu.VMEM((SUBLANES - 1, tm + SUBLANES, CONV_COLS), F32),
                        pltpu.VMEM((tm, D_MODEL), F32)],
        compiler_params=_params("arbitrary", "arbitrary"),
        name="conv_ffn",
    )(x, mod, ng, *[w[n] for n in names], w1, w2)


def _pool_kernel(x_ref, mod_ref, ng_ref, w_ref, b_ref, scale_ref, w1_ref, w2_ref, out_ref,
                 h_sc, s_a, s_b, y_sc):
    tm = TOKEN_TILE
    n_groups = len(POOL_WINDOWS)
    width = D_MODEL // n_groups
    mod = mod_ref[...]
    ng = ng_ref[...]
    x = x_ref[...]
    h = _rms(x, ng[0:1]) * (1.0 + mod[1:2]) + mod[0:1]

    @pl.when(pl.program_id(1) == 0)
    def _():
        h_sc[0:HALO, :] = jnp.zeros((HALO, D_MODEL), F32)

    @pl.when(pl.program_id(1) != 0)
    def _():
        h_sc[0:HALO, :] = h_sc[tm:tm + HALO, :]

    h_sc[HALO:HALO + tm, :] = h

    n = HALO + tm
    s_a[8:n, :] = h_sc[8:n, :] + h_sc[7:n - 1, :]
    s_b[16:n, width:] = s_a[16:n, width:] + s_a[14:n - 2, width:]
    s_a[24:n, 2 * width:] = s_b[24:n, 2 * width:] + s_b[20:n - 4, 2 * width:]
    s_b[32:n, 3 * width:] = s_a[32:n, 3 * width:] + s_a[24:n - 8, 3 * width:]

    t_idx = pl.program_id(1) * tm + lax.broadcasted_iota(jnp.int32, (tm, 1), 0)
    sums = (s_a, s_b, s_a, s_b)
    for g, win in enumerate(POOL_WINDOWS):
        cols = slice(g * width, (g + 1) * width)
        cnt = jnp.minimum(t_idx + 1, win).astype(F32)
        p = sums[g][HALO:n, cols] / cnt - h[:, cols]
        y_sc[:, cols] = _dot(p.astype(BF16), w_ref[g]) + b_ref[g:g + 1, :]

    y = y_sc[...] * scale_ref[...]
    x = x + mod[2:3] * _rms(y, ng[1:2])
    out_ref[...] = _ffn_sublayer(x, mod, ng, w1_ref, w2_ref)


def _pool_layer(x, mod, ng, w, b, scale, w1, w2, layer):
    batch, seq, _ = x.shape
    tm = TOKEN_TILE
    tile = pl.BlockSpec((None, tm, D_MODEL), lambda b_, t: (b_, t, 0))
    halo_buf = pltpu.VMEM((HALO + tm, D_MODEL), F32)
    return pl.pallas_call(
        _pool_kernel,
        grid=(batch, seq // tm),
        in_specs=[tile,
                  pl.BlockSpec((None, 6, D_MODEL), lambda b_, t: (b_, 0, 0)),
                  _const_spec((4, D_MODEL)),
                  _const_spec(w.shape), _const_spec(b.shape), _const_spec(scale.shape),
                  _layer_spec(w1, layer), _layer_spec(w2, layer)],
        out_specs=tile,
        out_shape=jax.ShapeDtypeStruct(x.shape, F32),
        scratch_shapes=[halo_buf, halo_buf, halo_buf, pltpu.VMEM((tm, D_MODEL), F32)],
        compiler_params=_params("arbitrary", "arbitrary"),
        name="pool_ffn",
    )(x, mod, ng, w, b, scale, w1, w2)


def _pad_heads(w, per_head, lane0=0):
    k = w.shape[0]
    w = w.reshape(k, MLA_HEADS, per_head)
    w = jnp.pad(w, ((0, 0), (0, 0), (lane0, HEAD_PAD - lane0 - per_head)))
    return w.reshape(k, MLA_HEADS * HEAD_PAD)


def _swap_halves(w):
    half = w.shape[-1] // 2
    return jnp.concatenate([w[..., half:], w[..., :half]], axis=-1)


def _mla_weights(w_dq, q_norm_g, w_uq, w_dkv, kv_norm_g, w_ukv, w_o):
    uq = w_uq.reshape(Q_LORA, MLA_HEADS, QK_NOPE + QK_ROPE)
    uq_nope = uq[..., :QK_NOPE].reshape(Q_LORA, -1)
    uq_rope = uq[..., QK_NOPE:]
    rope_cols = lambda w3: _pad_heads(w3.reshape(Q_LORA, -1), QK_ROPE, ROPE_LANE0)
    w_uq_pad = _pad_heads(uq_nope, QK_NOPE) + rope_cols(uq_rope)

    dkv_rope = w_dkv[:, KV_LORA:]
    pad_rope = lambda w: jnp.pad(w, ((0, 0), (ROPE_LANE0, LANES - ROPE_LANE0 - QK_ROPE)))
    w_dkv_all = jnp.concatenate(
        [w_dkv[:, :KV_LORA], pad_rope(dkv_rope), pad_rope(_swap_halves(dkv_rope))], axis=-1)

    ukv = w_ukv.reshape(KV_LORA, MLA_HEADS, QK_NOPE + V_HEAD)
    w_uk = _pad_heads(ukv[..., :QK_NOPE].reshape(KV_LORA, -1), QK_NOPE)
    w_uv = _pad_heads(ukv[..., QK_NOPE:].reshape(KV_LORA, -1), V_HEAD)

    return {
        "w_dq": w_dq.astype(BF16), "qg": q_norm_g.reshape(1, -1),
        "w_uq": w_uq_pad.astype(BF16),
        "w_dkv": w_dkv_all.astype(BF16), "kvg": kv_norm_g.reshape(1, -1),
        "w_uk": w_uk.astype(BF16), "w_uvt": w_uv.T.astype(BF16),
        "w_o": w_o.astype(BF16),
    }


def _rope_inv_freq_row():
    inv_freq = ROPE_THETA ** (-jnp.arange(0, QK_ROPE, 2, dtype=F32) / QK_ROPE)
    return jnp.tile(inv_freq, LANES // inv_freq.shape[0]).reshape(1, LANES)


def kernel(x, c, positions, ada_w, ada_b, norm_g, mla_w_dq, mla_q_norm_g, mla_w_uq, mla_w_dkv, mla_kv_norm_g, mla_w_ukv, mla_w_o, conv_w_pw1, conv_b_pw1, conv_w_dw, conv_b_dw, conv_ln_g, conv_ln_b, conv_w_pw2, conv_b_pw2, pool_w, pool_b, pool_scale, ffn_w1, ffn_w2):
    batch, seq, d = x.shape
    assert d == D_MODEL and seq % TOKEN_TILE == 0 and seq % ATTN_TILE == 0
    assert ATTN_TILE % CHUNK == 0 and TOKEN_TILE % CONV_ROWS == 0
    assert TOKEN_TILE == ATTN_TILE

    mod = _ada_mod(c, ada_w, ada_b)
    pos3 = positions.reshape(batch, seq, 1)
    invf = _rope_inv_freq_row()
    row = lambda v: v.reshape(1, -1)

    w1 = ffn_w1.astype(BF16)
    w2 = ffn_w2.astype(BF16)
    for i in range(DEPTH):
        kind = i % N_MIXERS
        j = i // N_MIXERS
        if kind == 0:
            w = _mla_weights(mla_w_dq[j], mla_q_norm_g[j], mla_w_uq[j], mla_w_dkv[j],
                             mla_kv_norm_g[j], mla_w_ukv[j], mla_w_o[j])
            q, k, vt = _mla_proj(x, pos3, mod[i], norm_g[i], invf, w)
            o = _attention(q, k, vt)
            x = _mla_out(x, o, mod[i], norm_g[i], w["w_o"], w1, w2, i)
        elif kind == 1:
            w = {
                "w_pw1": conv_w_pw1[j].astype(BF16), "b_pw1": row(conv_b_pw1[j]),
                "w_dw": conv_w_dw[j], "b_dw": row(conv_b_dw[j]),
                "ln_g": row(conv_ln_g[j]), "ln_b": row(conv_ln_b[j]),
                "w_pw2": conv_w_pw2[j].astype(BF16), "b_pw2": row(conv_b_pw2[j]),
            }
            x = _conv_layer(x, mod[i], norm_g[i], w, w1, w2, i)
        else:
            x = _pool_layer(x, mod[i], norm_g[i], pool_w[j].astype(BF16), pool_b[j],
                            row(pool_scale[j]), w1, w2, i)
    return x
```

```python
import functools
import math

import jax
import jax.numpy as jnp
from jax import lax
from jax.experimental import pallas as pl
from jax.experimental.pallas import tpu as pltpu

D_MODEL = 1024
DEPTH = 4
CHUNK = 64
N_MIXERS = 3
MLA_HEADS = 16
QK_NOPE = 64
QK_ROPE = 32
V_HEAD = 64
Q_LORA = 384
KV_LORA = 256
ROPE_THETA = 10000.0
CONV_WIDTH = 31
POOL_WINDOWS = (2, 4, 8, 16)
D_FF = 4 * D_MODEL
NORM_EPS = 1e-6
NEG_INF = -1e30

LANES = 128
HEAD_PAD = LANES
ROPE_LANE0 = QK_NOPE
MASK_LANE0 = QK_NOPE + QK_ROPE
ONES_ROW = V_HEAD
VMEM_LIMIT_BYTES = 56 * 1024 * 1024

TOKEN_TILE = 512
FF_CHUNK = 1024
ATTN_TILE = 512
ATTN_UNROLL = 12
ATTN_HEADS_PER_STEP = 2
SUBLANES = 8
CONV_ROWS = 64
CONV_COLS = 256
HALO = 32
ADA_COLS = 1536

F32 = jnp.float32
BF16 = jnp.bfloat16
_NT_DIMS = (((1,), (1,)), ((), ()))


def _const_spec(shape):
    zeros = (0,) * len(shape)
    return pl.BlockSpec(shape, lambda *_: zeros, pipeline_mode=pl.Buffered(1))


def _layer_spec(stacked, layer):
    index = (layer,) + (0,) * (stacked.ndim - 1)
    return pl.BlockSpec((None,) + stacked.shape[1:], lambda *_: index, pipeline_mode=pl.Buffered(1))


def _params(*semantics):
    return pltpu.CompilerParams(dimension_semantics=semantics,
                                vmem_limit_bytes=VMEM_LIMIT_BYTES)


def _dot(a, b):
    return jnp.dot(a, b, preferred_element_type=F32)


def _rms(x, g):
    return x * lax.rsqrt(jnp.mean(x * x, axis=-1, keepdims=True) + NORM_EPS) * g


def _ffn_sublayer(x, mod, ng, w1_ref, w2_ref):
    h = (_rms(x, ng[2:3]) * (1.0 + mod[4:5]) + mod[3:4]).astype(BF16)
    y = jnp.zeros(x.shape, F32)
    for c in range(D_FF // FF_CHUNK):
        cols = slice(c * FF_CHUNK, (c + 1) * FF_CHUNK)
        a = jnp.maximum(_dot(h, w1_ref[:, cols]), 0.0)
        y = y + _dot((a * a).astype(BF16), w2_ref[cols, :])
    return x + mod[5:6] * _rms(y, ng[3:4])


def _ada_kernel(c_ref, w_ref, b_ref, o_ref):
    c = c_ref[...]
    c_act = (c * jax.nn.sigmoid(c)).astype(BF16)
    o_ref[...] = _dot(c_act, w_ref[...].astype(BF16)) + b_ref[...]


def _ada_mod(c, ada_w, ada_b):
    batch = c.shape[0]
    n_cols = ada_w.shape[-1]
    out = pl.pallas_call(
        _ada_kernel,
        grid=(DEPTH, n_cols // ADA_COLS),
        in_specs=[
            pl.BlockSpec((batch, D_MODEL), lambda l, n: (0, 0)),
            pl.BlockSpec((None, D_MODEL, ADA_COLS), lambda l, n: (l, 0, n)),
            pl.BlockSpec((None, 1, ADA_COLS), lambda l, n: (l, 0, n)),
        ],
        out_specs=pl.BlockSpec((None, batch, ADA_COLS), lambda l, n: (l, 0, n)),
        out_shape=jax.ShapeDtypeStruct((DEPTH, batch, n_cols), F32),
        compiler_params=_params("arbitrary", "arbitrary"),
        name="ada_mod",
    )(c, ada_w, ada_b.reshape(DEPTH, 1, n_cols))
    return out.reshape(DEPTH, batch, 6, D_MODEL)


def _mla_proj_kernel(x_ref, pos_ref, mod_ref, ng_ref, invf_ref, w_dq_ref, qg_ref, w_uq_ref,
                     w_dkv_ref, kvg_ref, w_uk_ref, w_uvt_ref,
                     q_ref, k_ref, vt_ref):
    mod = mod_ref[...]
    ng = ng_ref[...]
    h = (_rms(x_ref[...], ng[0:1]) * (1.0 + mod[1:2]) + mod[0:1]).astype(BF16)

    ang = pos_ref[...].astype(F32) * invf_ref[...]
    lane = lax.broadcasted_iota(jnp.int32, (1, LANES), 1)
    first_half = (lane >= ROPE_LANE0) & (lane < ROPE_LANE0 + QK_ROPE // 2)
    second_half = (lane >= ROPE_LANE0 + QK_ROPE // 2) & (lane < ROPE_LANE0 + QK_ROPE)
    sin = jnp.sin(ang)
    cos_t = jnp.where(first_half | second_half, jnp.cos(ang), 1.0)
    sin_t = jnp.where(first_half, -sin, jnp.where(second_half, sin, 0.0))

    cq = _rms(_dot(h, w_dq_ref[...]), qg_ref[...]).astype(BF16)
    ckv_all = _dot(h, w_dkv_ref[...])
    ckv = _rms(ckv_all[:, :KV_LORA], kvg_ref[...]).astype(BF16)
    k_rope = (ckv_all[:, KV_LORA:KV_LORA + LANES] * cos_t
              + ckv_all[:, KV_LORA + LANES:] * sin_t)
    tile_rows = ckv_all.shape[0]
    chunk_in_block = (lax.broadcasted_iota(jnp.int32, (tile_rows, LANES), 0) // CHUNK) % (ATTN_TILE // CHUNK)
    chunk_lane = lax.broadcasted_iota(jnp.int32, (tile_rows, LANES), 1) - MASK_LANE0
    k_rope = k_rope + jnp.where(chunk_lane == chunk_in_block, 1.0, 0.0)

    scale = math.log2(math.e) / math.sqrt(QK_NOPE + QK_ROPE)
    group = 4 * HEAD_PAD
    row_in_head = lax.broadcasted_iota(jnp.int32, (group, 1), 0) % HEAD_PAD
    ones_row = jnp.where(row_in_head == ONES_ROW, 1.0, 0.0)
    for g in range(MLA_HEADS * HEAD_PAD // group):
        cols = slice(g * group, (g + 1) * group)
        q = _dot(cq, w_uq_ref[:, cols])
        k = _dot(ckv, w_uk_ref[:, cols])
        for j in range(group // HEAD_PAD):
            sl = slice(j * HEAD_PAD, (j + 1) * HEAD_PAD)
            out = slice(g * group + j * HEAD_PAD, g * group + (j + 1) * HEAD_PAD)
            qh = q[:, sl]
            half = QK_ROPE // 2
            q_swapped = jnp.where(first_half, pltpu.roll(qh, LANES - half, 1), pltpu.roll(qh, half, 1))
            q_ref[:, out] = ((qh * cos_t + q_swapped * sin_t) * scale).astype(BF16)
            k_ref[:, out] = (k[:, sl] + k_rope).astype(BF16)
        vt = lax.dot_general(w_uvt_ref[cols, :], ckv, _NT_DIMS, preferred_element_type=F32)
        vt_ref[cols, :] = (vt + ones_row).astype(BF16)


def _mla_proj(x, pos3, mod, ng, invf, w):
    batch, seq, _ = x.shape
    tm = TOKEN_TILE
    wide = MLA_HEADS * HEAD_PAD
    tile = lambda width: pl.BlockSpec((None, tm, width), lambda b, t: (b, t, 0))
    out_sds = jax.ShapeDtypeStruct((batch, seq, wide), BF16)
    vt_sds = jax.ShapeDtypeStruct((batch, seq // tm, wide, tm), BF16)
    vt_spec = pl.BlockSpec((None, None, wide, tm), lambda b, t: (b, t, 0, 0))
    return pl.pallas_call(
        _mla_proj_kernel,
        grid=(batch, seq // tm),
        in_specs=[
            tile(D_MODEL),
            tile(1),
            pl.BlockSpec((None, 6, D_MODEL), lambda b, t: (b, 0, 0)),
            _const_spec((4, D_MODEL)),
            _const_spec((1, LANES)),
            _const_spec(w["w_dq"].shape),
            _const_spec(w["qg"].shape),
            _const_spec(w["w_uq"].shape),
            _const_spec(w["w_dkv"].shape),
            _const_spec(w["kvg"].shape),
            _const_spec(w["w_uk"].shape),
            _const_spec(w["w_uvt"].shape),
        ],
        out_specs=[tile(wide), tile(wide), vt_spec],
        out_shape=[out_sds, out_sds, vt_sds],
        compiler_params=_params("arbitrary", "arbitrary"),
        name="mla_proj",
    )(x, pos3, mod, ng, invf, w["w_dq"], w["qg"], w["w_uq"], w["w_dkv"], w["kvg"],
      w["w_uk"], w["w_uvt"])


def _next_block(i, j):
    wrap = j >= i
    return jnp.where(wrap, i + 1, i), jnp.where(wrap, 0, j + 1)


def _attn_kernel(q_ref, k_ref, vt_ref, ot_ref, m_sc, acc_sc, feat_sc, *stage_sc):
    t = ATTN_TILE
    heads = [slice(hh * HEAD_PAD, (hh + 1) * HEAD_PAD) for hh in range(ATTN_HEADS_PER_STEP)]
    n_tiles = q_ref.shape[0] // t
    n_blocks = n_tiles * (n_tiles + 1) // 2
    n_heads = len(heads)
    per_kind = 2 * n_heads
    by_slot = lambda refs: [refs[slot * n_heads:(slot + 1) * n_heads] for slot in range(2)]
    s_sc, smax_sc = (by_slot(stage_sc[kind * per_kind:(kind + 1) * per_kind]) for kind in range(2))

    row_chunk = lax.broadcasted_iota(jnp.int32, (t, HEAD_PAD), 0) // CHUNK
    later_chunk = lax.broadcasted_iota(jnp.int32, (t, HEAD_PAD), 1) - MASK_LANE0
    masked_lane = (later_chunk > row_chunk) & (later_chunk < t // CHUNK)
    feat_sc[0] = jnp.zeros((t, HEAD_PAD), BF16)
    feat_sc[1] = jnp.where(masked_lane, NEG_INF, 0.0).astype(BF16)
    acc_sc[...] = jnp.zeros(acc_sc.shape, F32)

    def rows(idx):
        return pl.ds(pl.multiple_of(idx * t, t), t)

    def scores(blk, slot):
        i, j = blk
        feat = feat_sc[(i == j).astype(jnp.int32)]
        for hh, head in enumerate(heads):
            s = lax.dot_general(k_ref[rows(j), head], q_ref[rows(i), head] + feat,
                                _NT_DIMS, preferred_element_type=F32)
            s_sc[slot][hh][...] = s
            smax_sc[slot][hh][...] = jnp.max(s, axis=0, keepdims=True)

    def softmax_values(blk, slot):
        i, j = blk
        for hh, head in enumerate(heads):
            m_old = jnp.where(j == 0, NEG_INF, m_sc[hh:hh + 1, :])
            m_new = jnp.maximum(m_old, smax_sc[slot][hh][...])
            p = jnp.exp2(s_sc[slot][hh][...] - m_new).astype(BF16)
            m_sc[hh:hh + 1, :] = m_new
            acc = jnp.exp2(m_old - m_new) * acc_sc[head, :] + _dot(vt_ref[j, head, :], p)
            acc_sc[head, :] = acc
            inv_sum = 1.0 / acc[ONES_ROW:ONES_ROW + 1, :]
            ot_ref[i, hh * V_HEAD:(hh + 1) * V_HEAD, :] = (acc[:V_HEAD, :] * inv_sum).astype(BF16)

    def run_blocks(blk, prefetch_last):
        for u in range(ATTN_UNROLL):
            nxt = _next_block(*blk)
            if u + 1 < ATTN_UNROLL or prefetch_last:
                scores(nxt, (u + 1) % 2)
            softmax_values(blk, u % 2)
            blk = nxt
        return blk

    zero = jnp.int32(0)
    scores((zero, zero), 0)
    assert ATTN_UNROLL % 2 == 0 and n_blocks % ATTN_UNROLL == 0
    blk = lax.fori_loop(0, n_blocks // ATTN_UNROLL - 1, lambda _, b: run_blocks(b, True), (zero, zero))
    run_blocks(blk, False)


def _attention(q, k, vt):
    batch, seq, _ = q.shape
    t = ATTN_TILE
    n = ATTN_HEADS_PER_STEP
    width = n * HEAD_PAD
    seq_spec = pl.BlockSpec((None, seq, width), lambda b, h: (b, 0, h))
    transposed_spec = pl.BlockSpec((None, seq // t, width, t), lambda b, h: (b, 0, h, 0))
    return pl.pallas_call(
        _attn_kernel,
        grid=(batch, MLA_HEADS // n),
        in_specs=[seq_spec, seq_spec, transposed_spec],
        out_specs=pl.BlockSpec((None, seq // t, n * V_HEAD, t), lambda b, h: (b, 0, h, 0)),
        out_shape=jax.ShapeDtypeStruct((batch, seq // t, MLA_HEADS * V_HEAD, t), BF16),
        scratch_shapes=[
            pltpu.VMEM((n, t), F32),
            pltpu.VMEM((width, t), F32),
            pltpu.VMEM((2, t, HEAD_PAD), BF16),
        ]
        + [pltpu.VMEM((t, t), F32)] * (2 * n) + [pltpu.VMEM((1, t), F32)] * (2 * n),
        compiler_params=_params("arbitrary", "arbitrary"),
        name="mla_attention",
    )(q, k, vt)


def _mla_out_kernel(x_ref, ot_ref, mod_ref, ng_ref, w_o_ref, w1_ref, w2_ref, out_ref):
    mod = mod_ref[...]
    ng = ng_ref[...]
    y = lax.dot_general(ot_ref[...], w_o_ref[...], (((0,), (0,)), ((), ())),
                        preferred_element_type=F32)
    x = x_ref[...] + mod[2:3] * _rms(y, ng[1:2])
    out_ref[...] = _ffn_sublayer(x, mod, ng, w1_ref, w2_ref)


def _mla_out(x, o, mod, ng, w_o, w1, w2, layer):
    batch, seq, _ = x.shape
    tm = TOKEN_TILE
    tile = lambda width: pl.BlockSpec((None, tm, width), lambda b, t: (b, t, 0))
    return pl.pallas_call(
        _mla_out_kernel,
        grid=(batch, seq // tm),
        in_specs=[
            tile(D_MODEL),
            pl.BlockSpec((None, None, o.shape[2], tm), lambda b, t: (b, t, 0, 0)),
            pl.BlockSpec((None, 6, D_MODEL), lambda b, t: (b, 0, 0)),
            _const_spec((4, D_MODEL)),
            _const_spec(w_o.shape),
            _layer_spec(w1, layer),
            _layer_spec(w2, layer),
        ],
        out_specs=tile(D_MODEL),
        out_shape=jax.ShapeDtypeStruct(x.shape, F32),
        compiler_params=_params("arbitrary", "arbitrary"),
        name="mla_out_ffn",
    )(x, o, mod, ng, w_o, w1, w2)


def _conv_kernel(x_ref, mod_ref, ng_ref, w_pw1_ref, b_pw1_ref, w_dw_ref, b_dw_ref, ln_g_ref,
                 ln_b_ref, w_pw2_ref, b_pw2_ref, w1_ref, w2_ref, out_ref, u_sc, v_sc, c_sc):
    tm = TOKEN_TILE
    mod = mod_ref[...]
    ng = ng_ref[...]
    x = x_ref[...]
    h = (_rms(x, ng[0:1]) * (1.0 + mod[1:2]) + mod[0:1]).astype(BF16)
    a = _dot(h, w_pw1_ref[...]) + b_pw1_ref[...]
    u = a[:, :D_MODEL] * jax.nn.sigmoid(a[:, D_MODEL:])

    @pl.when(pl.program_id(1) == 0)
    def _():
        u_sc[0:HALO, :] = jnp.zeros((HALO, D_MODEL), F32)

    @pl.when(pl.program_id(1) != 0)
    def _():
        u_sc[0:HALO, :] = u_sc[tm:tm + HALO, :]

    u_sc[HALO:HALO + tm, :] = u

    first_tap = HALO - (CONV_WIDTH - 1)
    taps = [[] for _ in range(SUBLANES)]
    for j in range(CONV_WIDTH):
        taps[(first_tap + j) % SUBLANES].append(((first_tap + j) // SUBLANES, j))
    w_dw = w_dw_ref[...]
    b_dw = b_dw_ref[...]

    for c0 in range(0, D_MODEL, CONV_COLS):
        cols = slice(c0, c0 + CONV_COLS)

        def partial_sum(r, r0, nrows):
            return sum(u_sc[r0 + SUBLANES * a:r0 + SUBLANES * a + nrows, cols] * w_dw[j:j + 1, cols]
                       for a, j in taps[r])

        for r0 in range(0, tm + SUBLANES, CONV_ROWS):
            nrows = min(CONV_ROWS, tm + SUBLANES - r0)
            for r in range(1, SUBLANES):
                v_sc[r - 1, r0:r0 + nrows, :] = partial_sum(r, r0, nrows)
        for r0 in range(0, tm, CONV_ROWS):
            acc = b_dw[:, cols] + partial_sum(0, r0, CONV_ROWS)
            for r in range(1, SUBLANES):
                acc = acc + v_sc[r - 1, r0 + r:r0 + r + CONV_ROWS, :]
            c_sc[r0:r0 + CONV_ROWS, cols] = acc

    cv = c_sc[...]
    mu = jnp.mean(cv, axis=-1, keepdims=True)
    xc = cv - mu
    ln = xc * lax.rsqrt(jnp.mean(xc * xc, axis=-1, keepdims=True) + NORM_EPS)
    ln = ln * ln_g_ref[...] + ln_b_ref[...]
    act = (ln * jax.nn.sigmoid(ln)).astype(BF16)
    y = _dot(act, w_pw2_ref[...]) + b_pw2_ref[...]
    x = x + mod[2:3] * _rms(y, ng[1:2])
    out_ref[...] = _ffn_sublayer(x, mod, ng, w1_ref, w2_ref)


def _conv_layer(x, mod, ng, w, w1, w2, layer):
    batch, seq, _ = x.shape
    tm = TOKEN_TILE
    tile = pl.BlockSpec((None, tm, D_MODEL), lambda b, t: (b, t, 0))
    names = ("w_pw1", "b_pw1", "w_dw", "b_dw", "ln_g", "ln_b", "w_pw2", "b_pw2")
    return pl.pallas_call(
        _conv_kernel,
        grid=(batch, seq // tm),
        in_specs=[tile,
                  pl.BlockSpec((None, 6, D_MODEL), lambda b, t: (b, 0, 0)),
                  _const_spec((4, D_MODEL))]
                 + [_const_spec(w[n].shape) for n in names]
                 + [_layer_spec(w1, layer), _layer_spec(w2, layer)],
        out_specs=tile,
        out_shape=jax.ShapeDtypeStruct(x.shape, F32),
        scratch_shapes=[pltpu.VMEM((HALO + tm, D_MODEL), F32),
                        pltpu.VMEM((SUBLANES - 1, tm + SUBLANES, CONV_COLS), F32),
                        pltpu.VMEM((tm, D_MODEL), F32)],
        compiler_params=_params("arbitrary", "arbitrary"),
        name="conv_ffn",
    )(x, mod, ng, *[w[n] for n in names], w1, w2)


def _pool_kernel(x_ref, mod_ref, ng_ref, w_ref, b_ref, scale_ref, w1_ref, w2_ref, out_ref,
                 h_sc, s_a, s_b, y_sc):
    tm = TOKEN_TILE
    n_groups = len(POOL_WINDOWS)
    width = D_MODEL // n_groups
    mod = mod_ref[...]
    ng = ng_ref[...]
    x = x_ref[...]
    h = _rms(x, ng[0:1]) * (1.0 + mod[1:2]) + mod[0:1]

    @pl.when(pl.program_id(1) == 0)
    def _():
        h_sc[0:HALO, :] = jnp.zeros((HALO, D_MODEL), F32)

    @pl.when(pl.program_id(1) != 0)
    def _():
        h_sc[0:HALO, :] = h_sc[tm:tm + HALO, :]

    h_sc[HALO:HALO + tm, :] = h

    n = HALO + tm
    s_a[8:n, :] = h_sc[8:n, :] + h_sc[7:n - 1, :]
    s_b[16:n, width:] = s_a[16:n, width:] + s_a[14:n - 2, width:]
    s_a[24:n, 2 * width:] = s_b[24:n, 2 * width:] + s_b[20:n - 4, 2 * width:]
    s_b[32:n, 3 * width:] = s_a[32:n, 3 * width:] + s_a[24:n - 8, 3 * width:]

    t_idx = pl.program_id(1) * tm + lax.broadcasted_iota(jnp.int32, (tm, 1), 0)
    sums = (s_a, s_b, s_a, s_b)
    for g, win in enumerate(POOL_WINDOWS):
        cols = slice(g * width, (g + 1) * width)
        cnt = jnp.minimum(t_idx + 1, win).astype(F32)
        p = sums[g][HALO:n, cols] / cnt - h[:, cols]
        y_sc[:, cols] = _dot(p.astype(BF16), w_ref[g]) + b_ref[g:g + 1, :]

    y = y_sc[...] * scale_ref[...]
    x = x + mod[2:3] * _rms(y, ng[1:2])
    out_ref[...] = _ffn_sublayer(x, mod, ng, w1_ref, w2_ref)


def _pool_layer(x, mod, ng, w, b, scale, w1, w2, layer):
    batch, seq, _ = x.shape
    tm = TOKEN_TILE
    tile = pl.BlockSpec((None, tm, D_MODEL), lambda b_, t: (b_, t, 0))
    halo_buf = pltpu.VMEM((HALO + tm, D_MODEL), F32)
    return pl.pallas_call(
        _pool_kernel,
        grid=(batch, seq // tm),
        in_specs=[tile,
                  pl.BlockSpec((None, 6, D_MODEL), lambda b_, t: (b_, 0, 0)),
                  _const_spec((4, D_MODEL)),
                  _const_spec(w.shape), _const_spec(b.shape), _const_spec(scale.shape),
                  _layer_spec(w1, layer), _layer_spec(w2, layer)],
        out_specs=tile,
        out_shape=jax.ShapeDtypeStruct(x.shape, F32),
        scratch_shapes=[halo_buf, halo_buf, halo_buf, pltpu.VMEM((tm, D_MODEL), F32)],
        compiler_params=_params("arbitrary", "arbitrary"),
        name="pool_ffn",
    )(x, mod, ng, w, b, scale, w1, w2)


def _pad_heads(w, per_head, lane0=0):
    k = w.shape[0]
    w = w.reshape(k, MLA_HEADS, per_head)
    w = jnp.pad(w, ((0, 0), (0, 0), (lane0, HEAD_PAD - lane0 - per_head)))
    return w.reshape(k, MLA_HEADS * HEAD_PAD)


def _swap_halves(w):
    half = w.shape[-1] // 2
    return jnp.concatenate([w[..., half:], w[..., :half]], axis=-1)


def _mla_weights(w_dq, q_norm_g, w_uq, w_dkv, kv_norm_g, w_ukv, w_o):
    uq = w_uq.reshape(Q_LORA, MLA_HEADS, QK_NOPE + QK_ROPE)
    uq_nope = uq[..., :QK_NOPE].reshape(Q_LORA, -1)
    uq_rope = uq[..., QK_NOPE:]
    rope_cols = lambda w3: _pad_heads(w3.reshape(Q_LORA, -1), QK_ROPE, ROPE_LANE0)
    w_uq_pad = _pad_heads(uq_nope, QK_NOPE) + rope_cols(uq_rope)

    dkv_rope = w_dkv[:, KV_LORA:]
    pad_rope = lambda w: jnp.pad(w, ((0, 0), (ROPE_LANE0, LANES - ROPE_LANE0 - QK_ROPE)))
    w_dkv_all = jnp.concatenate(
        [w_dkv[:, :KV_LORA], pad_rope(dkv_rope), pad_rope(_swap_halves(dkv_rope))], axis=-1)

    ukv = w_ukv.reshape(KV_LORA, MLA_HEADS, QK_NOPE + V_HEAD)
    w_uk = _pad_heads(ukv[..., :QK_NOPE].reshape(KV_LORA, -1), QK_NOPE)
    w_uv = _pad_heads(ukv[..., QK_NOPE:].reshape(KV_LORA, -1), V_HEAD)

    return {
        "w_dq": w_dq.astype(BF16), "qg": q_norm_g.reshape(1, -1),
        "w_uq": w_uq_pad.astype(BF16),
        "w_dkv": w_dkv_all.astype(BF16), "kvg": kv_norm_g.reshape(1, -1),
        "w_uk": w_uk.astype(BF16), "w_uvt": w_uv.T.astype(BF16),
        "w_o": w_o.astype(BF16),
    }


def _rope_inv_freq_row():
    inv_freq = ROPE_THETA ** (-jnp.arange(0, QK_ROPE, 2, dtype=F32) / QK_ROPE)
    return jnp.tile(inv_freq, LANES // inv_freq.shape[0]).reshape(1, LANES)


def kernel(x, c, positions, ada_w, ada_b, norm_g, mla_w_dq, mla_q_norm_g, mla_w_uq, mla_w_dkv, mla_kv_norm_g, mla_w_ukv, mla_w_o, conv_w_pw1, conv_b_pw1, conv_w_dw, conv_b_dw, conv_ln_g, conv_ln_b, conv_w_pw2, conv_b_pw2, pool_w, pool_b, pool_scale, ffn_w1, ffn_w2):
    batch, seq, d = x.shape
    assert d == D_MODEL and seq % TOKEN_TILE == 0 and seq % ATTN_TILE == 0
    assert ATTN_TILE % CHUNK == 0 and TOKEN_TILE % CONV_ROWS == 0
    assert TOKEN_TILE == ATTN_TILE

    mod = _ada_mod(c, ada_w, ada_b)
    pos3 = positions.reshape(batch, seq, 1)
    invf = _rope_inv_freq_row()
    row = lambda v: v.reshape(1, -1)

    w1 = ffn_w1.astype(BF16)
    w2 = ffn_w2.astype(BF16)
    for i in range(DEPTH):
        kind = i % N_MIXERS
        j = i // N_MIXERS
        if kind == 0:
            w = _mla_weights(mla_w_dq[j], mla_q_norm_g[j], mla_w_uq[j], mla_w_dkv[j],
                             mla_kv_norm_g[j], mla_w_ukv[j], mla_w_o[j])
            q, k, vt = _mla_proj(x, pos3, mod[i], norm_g[i], invf, w)
            o = _attention(q, k, vt)
            x = _mla_out(x, o, mod[i], norm_g[i], w["w_o"], w1, w2, i)
        elif kind == 1:
            w = {
                "w_pw1": conv_w_pw1[j].astype(BF16), "b_pw1": row(conv_b_pw1[j]),
                "w_dw": conv_w_dw[j], "b_dw": row(conv_b_dw[j]),
                "ln_g": row(conv_ln_g[j]), "ln_b": row(conv_ln_b[j]),
                "w_pw2": conv_w_pw2[j].astype(BF16), "b_pw2": row(conv_b_pw2[j]),
            }
            x = _conv_layer(x, mod[i], norm_g[i], w, w1, w2, i)
        else:
            x = _pool_layer(x, mod[i], norm_g[i], pool_w[j].astype(BF16), pool_b[j],
                            row(pool_scale[j]), w1, w2, i)
    return x
```

```python
import functools
import math

import jax
import jax.numpy as jnp
from jax import lax
from jax.experimental import pallas as pl
from jax.experimental.pallas import tpu as pltpu

D_MODEL = 1024
DEPTH = 4
CHUNK = 64
N_MIXERS = 3
MLA_HEADS = 16
QK_NOPE = 64
QK_ROPE = 32
V_HEAD = 64
Q_LORA = 384
KV_LORA = 256
ROPE_THETA = 10000.0
CONV_WIDTH = 31
POOL_WINDOWS = (2, 4, 8, 16)
D_FF = 4 * D_MODEL
NORM_EPS = 1e-6
NEG_INF = -1e30

LANES = 128
HEAD_PAD = LANES
ROPE_LANE0 = QK_NOPE
MASK_LANE0 = QK_NOPE + QK_ROPE
ONES_ROW = V_HEAD
VMEM_LIMIT_BYTES = 56 * 1024 * 1024

TOKEN_TILE = 512
FF_ROWS = 256
FF_CHUNK = 1024
ATTN_TILE = 512
ATTN_UNROLL = 12
ATTN_HEADS_PER_STEP = 2
SUBLANES = 8
CONV_ROWS = 64
CONV_COLS = 256
HALO = 32
ADA_COLS = 1536

F32 = jnp.float32
BF16 = jnp.bfloat16
_NT_DIMS = (((1,), (1,)), ((), ()))


def _const_spec(shape):
    zeros = (0,) * len(shape)
    return pl.BlockSpec(shape, lambda *_: zeros, pipeline_mode=pl.Buffered(1))


def _layer_spec(stacked, layer):
    index = (layer,) + (0,) * (stacked.ndim - 1)
    return pl.BlockSpec((None,) + stacked.shape[1:], lambda *_: index, pipeline_mode=pl.Buffered(1))


def _params(*semantics):
    return pltpu.CompilerParams(dimension_semantics=semantics,
                                vmem_limit_bytes=VMEM_LIMIT_BYTES)


def _dot(a, b):
    return jnp.dot(a, b, preferred_element_type=F32)


def _rms(x, g):
    return x * lax.rsqrt(jnp.mean(x * x, axis=-1, keepdims=True) + NORM_EPS) * g


def _ffn_sublayer(x, mod, ng, w1_ref, w2_ref):
    outs = []
    for r0 in range(0, x.shape[0], FF_ROWS):
        xr = x[r0:r0 + FF_ROWS, :]
        h = (_rms(xr, ng[2:3]) * (1.0 + mod[4:5]) + mod[3:4]).astype(BF16)
        y = jnp.zeros(xr.shape, F32)
        for c in range(D_FF // FF_CHUNK):
            cols = slice(c * FF_CHUNK, (c + 1) * FF_CHUNK)
            a = jnp.maximum(_dot(h, w1_ref[:, cols]), 0.0)
            y = y + _dot((a * a).astype(BF16), w2_ref[cols, :])
        outs.append(xr + mod[5:6] * _rms(y, ng[3:4]))
    return jnp.concatenate(outs, axis=0)


def _ada_kernel(c_ref, w_ref, b_ref, o_ref):
    c = c_ref[...]
    c_act = (c * jax.nn.sigmoid(c)).astype(BF16)
    o_ref[...] = _dot(c_act, w_ref[...].astype(BF16)) + b_ref[...]


def _ada_mod(c, ada_w, ada_b):
    batch = c.shape[0]
    n_cols = ada_w.shape[-1]
    out = pl.pallas_call(
        _ada_kernel,
        grid=(DEPTH, n_cols // ADA_COLS),
        in_specs=[
            pl.BlockSpec((batch, D_MODEL), lambda l, n: (0, 0)),
            pl.BlockSpec((None, D_MODEL, ADA_COLS), lambda l, n: (l, 0, n)),
            pl.BlockSpec((None, 1, ADA_COLS), lambda l, n: (l, 0, n)),
        ],
        out_specs=pl.BlockSpec((None, batch, ADA_COLS), lambda l, n: (l, 0, n)),
        out_shape=jax.ShapeDtypeStruct((DEPTH, batch, n_cols), F32),
        compiler_params=_params("arbitrary", "arbitrary"),
        name="ada_mod",
    )(c, ada_w, ada_b.reshape(DEPTH, 1, n_cols))
    return out.reshape(DEPTH, batch, 6, D_MODEL)


def _mla_proj_kernel(x_ref, pos_ref, mod_ref, ng_ref, invf_ref, w_dq_ref, qg_ref, w_uq_ref,
                     w_dkv_ref, kvg_ref, w_uk_ref, w_uvt_ref,
                     q_ref, k_ref, vt_ref):
    mod = mod_ref[...]
    ng = ng_ref[...]
    h = (_rms(x_ref[...], ng[0:1]) * (1.0 + mod[1:2]) + mod[0:1]).astype(BF16)

    ang = pos_ref[...].astype(F32) * invf_ref[...]
    lane = lax.broadcasted_iota(jnp.int32, (1, LANES), 1)
    first_half = (lane >= ROPE_LANE0) & (lane < ROPE_LANE0 + QK_ROPE // 2)
    second_half = (lane >= ROPE_LANE0 + QK_ROPE // 2) & (lane < ROPE_LANE0 + QK_ROPE)
    sin = jnp.sin(ang)
    cos_t = jnp.where(first_half | second_half, jnp.cos(ang), 1.0)
    sin_t = jnp.where(first_half, -sin, jnp.where(second_half, sin, 0.0))

    cq = _rms(_dot(h, w_dq_ref[...]), qg_ref[...]).astype(BF16)
    ckv_all = _dot(h, w_dkv_ref[...])
    ckv = _rms(ckv_all[:, :KV_LORA], kvg_ref[...]).astype(BF16)
    k_rope = (ckv_all[:, KV_LORA:KV_LORA + LANES] * cos_t
              + ckv_all[:, KV_LORA + LANES:] * sin_t)
    tile_rows = ckv_all.shape[0]
    chunk_in_block = (lax.broadcasted_iota(jnp.int32, (tile_rows, LANES), 0) // CHUNK) % (ATTN_TILE // CHUNK)
    chunk_lane = lax.broadcasted_iota(jnp.int32, (tile_rows, LANES), 1) - MASK_LANE0
    k_rope = k_rope + jnp.where(chunk_lane == chunk_in_block, 1.0, 0.0)

    scale = math.log2(math.e) / math.sqrt(QK_NOPE + QK_ROPE)
    group = 4 * HEAD_PAD
    row_in_head = lax.broadcasted_iota(jnp.int32, (group, 1), 0) % HEAD_PAD
    ones_row = jnp.where(row_in_head == ONES_ROW, 1.0, 0.0)
    for g in range(MLA_HEADS * HEAD_PAD // group):
        cols = slice(g * group, (g + 1) * group)
        q = _dot(cq, w_uq_ref[:, cols])
        k = _dot(ckv, w_uk_ref[:, cols])
        for j in range(group // HEAD_PAD):
            sl = slice(j * HEAD_PAD, (j + 1) * HEAD_PAD)
            out = slice(g * group + j * HEAD_PAD, g * group + (j + 1) * HEAD_PAD)
            qh = q[:, sl]
            half = QK_ROPE // 2
            q_swapped = jnp.where(first_half, pltpu.roll(qh, LANES - half, 1), pltpu.roll(qh, half, 1))
            q_ref[:, out] = ((qh * cos_t + q_swapped * sin_t) * scale).astype(BF16)
            k_ref[:, out] = (k[:, sl] + k_rope).astype(BF16)
        vt = lax.dot_general(w_uvt_ref[cols, :], ckv, _NT_DIMS, preferred_element_type=F32)
        vt_ref[cols, :] = (vt + ones_row).astype(BF16)


def _mla_proj(x, pos3, mod, ng, invf, w):
    batch, seq, _ = x.shape
    tm = TOKEN_TILE
    wide = MLA_HEADS * HEAD_PAD
    tile = lambda width: pl.BlockSpec((None, tm, width), lambda b, t: (b, t, 0))
    out_sds = jax.ShapeDtypeStruct((batch, seq, wide), BF16)
    vt_sds = jax.ShapeDtypeStruct((batch, seq // tm, wide, tm), BF16)
    vt_spec = pl.BlockSpec((None, None, wide, tm), lambda b, t: (b, t, 0, 0))
    return pl.pallas_call(
        _mla_proj_kernel,
        grid=(batch, seq // tm),
        in_specs=[
            tile(D_MODEL),
            tile(1),
            pl.BlockSpec((None, 6, D_MODEL), lambda b, t: (b, 0, 0)),
            _const_spec((4, D_MODEL)),
            _const_spec((1, LANES)),
            _const_spec(w["w_dq"].shape),
            _const_spec(w["qg"].shape),
            _const_spec(w["w_uq"].shape),
            _const_spec(w["w_dkv"].shape),
            _const_spec(w["kvg"].shape),
            _const_spec(w["w_uk"].shape),
            _const_spec(w["w_uvt"].shape),
        ],
        out_specs=[tile(wide), tile(wide), vt_spec],
        out_shape=[out_sds, out_sds, vt_sds],
        compiler_params=_params("arbitrary", "arbitrary"),
        name="mla_proj",
    )(x, pos3, mod, ng, invf, w["w_dq"], w["qg"], w["w_uq"], w["w_dkv"], w["kvg"],
      w["w_uk"], w["w_uvt"])


def _next_block(i, j):
    wrap = j >= i
    return jnp.where(wrap, i + 1, i), jnp.where(wrap, 0, j + 1)


def _attn_kernel(q_ref, k_ref, vt_ref, ot_ref, m_sc, acc_sc, feat_sc, *stage_sc):
    t = ATTN_TILE
    heads = [slice(hh * HEAD_PAD, (hh + 1) * HEAD_PAD) for hh in range(ATTN_HEADS_PER_STEP)]
    n_tiles = q_ref.shape[0] // t
    n_blocks = n_tiles * (n_tiles + 1) // 2
    n_heads = len(heads)
    per_kind = 2 * n_heads
    by_slot = lambda refs: [refs[slot * n_heads:(slot + 1) * n_heads] for slot in range(2)]
    s_sc, smax_sc = (by_slot(stage_sc[kind * per_kind:(kind + 1) * per_kind]) for kind in range(2))

    row_chunk = lax.broadcasted_iota(jnp.int32, (t, HEAD_PAD), 0) // CHUNK
    later_chunk = lax.broadcasted_iota(jnp.int32, (t, HEAD_PAD), 1) - MASK_LANE0
    masked_lane = (later_chunk > row_chunk) & (later_chunk < t // CHUNK)
    feat_sc[0] = jnp.zeros((t, HEAD_PAD), BF16)
    feat_sc[1] = jnp.where(masked_lane, NEG_INF, 0.0).astype(BF16)
    acc_sc[...] = jnp.zeros(acc_sc.shape, F32)

    def rows(idx):
        return pl.ds(pl.multiple_of(idx * t, t), t)

    def scores(blk, slot):
        i, j = blk
        feat = feat_sc[(i == j).astype(jnp.int32)]
        for hh, head in enumerate(heads):
            s = lax.dot_general(k_ref[rows(j), head], q_ref[rows(i), head] + feat,
                                _NT_DIMS, preferred_element_type=F32)
            s_sc[slot][hh][...] = s
            smax_sc[slot][hh][...] = jnp.max(s, axis=0, keepdims=True)

    def softmax_values(blk, slot):
        i, j = blk
        for hh, head in enumerate(heads):
            m_old = jnp.where(j == 0, NEG_INF, m_sc[hh:hh + 1, :])
            m_new = jnp.maximum(m_old, smax_sc[slot][hh][...])
            p = jnp.exp2(s_sc[slot][hh][...] - m_new).astype(BF16)
            m_sc[hh:hh + 1, :] = m_new
            acc = jnp.exp2(m_old - m_new) * acc_sc[head, :] + _dot(vt_ref[j, head, :], p)
            acc_sc[head, :] = acc
            inv_sum = 1.0 / acc[ONES_ROW:ONES_ROW + 1, :]
            ot_ref[i, hh * V_HEAD:(hh + 1) * V_HEAD, :] = (acc[:V_HEAD, :] * inv_sum).astype(BF16)

    def run_blocks(blk, prefetch_last):
        for u in range(ATTN_UNROLL):
            nxt = _next_block(*blk)
            if u + 1 < ATTN_UNROLL or prefetch_last:
                scores(nxt, (u + 1) % 2)
            softmax_values(blk, u % 2)
            blk = nxt
        return blk

    zero = jnp.int32(0)
    scores((zero, zero), 0)
    assert ATTN_UNROLL % 2 == 0 and n_blocks % ATTN_UNROLL == 0
    blk = lax.fori_loop(0, n_blocks // ATTN_UNROLL - 1, lambda _, b: run_blocks(b, True), (zero, zero))
    run_blocks(blk, False)


def _attention(q, k, vt):
    batch, seq, _ = q.shape
    t = ATTN_TILE
    n = ATTN_HEADS_PER_STEP
    width = n * HEAD_PAD
    seq_spec = pl.BlockSpec((None, seq, width), lambda b, h: (b, 0, h))
    transposed_spec = pl.BlockSpec((None, seq // t, width, t), lambda b, h: (b, 0, h, 0))
    return pl.pallas_call(
        _attn_kernel,
        grid=(batch, MLA_HEADS // n),
        in_specs=[seq_spec, seq_spec, transposed_spec],
        out_specs=pl.BlockSpec((None, seq // t, n * V_HEAD, t), lambda b, h: (b, 0, h, 0)),
        out_shape=jax.ShapeDtypeStruct((batch, seq // t, MLA_HEADS * V_HEAD, t), BF16),
        scratch_shapes=[
            pltpu.VMEM((n, t), F32),
            pltpu.VMEM((width, t), F32),
            pltpu.VMEM((2, t, HEAD_PAD), BF16),
        ]
        + [pltpu.VMEM((t, t), F32)] * (2 * n) + [pltpu.VMEM((1, t), F32)] * (2 * n),
        compiler_params=_params("arbitrary", "arbitrary"),
        name="mla_attention",
    )(q, k, vt)


def _mla_out_kernel(x_ref, ot_ref, mod_ref, ng_ref, w_o_ref, w1_ref, w2_ref, out_ref):
    mod = mod_ref[...]
    ng = ng_ref[...]
    y = lax.dot_general(ot_ref[...], w_o_ref[...], (((0,), (0,)), ((), ())),
                        preferred_element_type=F32)
    x = x_ref[...] + mod[2:3] * _rms(y, ng[1:2])
    out_ref[...] = _ffn_sublayer(x, mod, ng, w1_ref, w2_ref)


def _mla_out(x, o, mod, ng, w_o, w1, w2, layer):
    batch, seq, _ = x.shape
    tm = TOKEN_TILE
    tile = lambda width: pl.BlockSpec((None, tm, width), lambda b, t: (b, t, 0))
    return pl.pallas_call(
        _mla_out_kernel,
        grid=(batch, seq // tm),
        in_specs=[
            tile(D_MODEL),
            pl.BlockSpec((None, None, o.shape[2], tm), lambda b, t: (b, t, 0, 0)),
            pl.BlockSpec((None, 6, D_MODEL), lambda b, t: (b, 0, 0)),
            _const_spec((4, D_MODEL)),
            _const_spec(w_o.shape),
            _layer_spec(w1, layer),
            _layer_spec(w2, layer),
        ],
        out_specs=tile(D_MODEL),
        out_shape=jax.ShapeDtypeStruct(x.shape, F32),
        compiler_params=_params("arbitrary", "arbitrary"),
        name="mla_out_ffn",
    )(x, o, mod, ng, w_o, w1, w2)


def _conv_kernel(x_ref, mod_ref, ng_ref, w_pw1_ref, b_pw1_ref, w_dw_ref, b_dw_ref, ln_g_ref,
                 ln_b_ref, w_pw2_ref, b_pw2_ref, w1_ref, w2_ref, out_ref, u_sc, v_sc, c_sc):
    tm = TOKEN_TILE
    mod = mod_ref[...]
    ng = ng_ref[...]
    x = x_ref[...]
    h = (_rms(x, ng[0:1]) * (1.0 + mod[1:2]) + mod[0:1]).astype(BF16)
    a = _dot(h, w_pw1_ref[...]) + b_pw1_ref[...]
    u = a[:, :D_MODEL] * jax.nn.sigmoid(a[:, D_MODEL:])

    @pl.when(pl.program_id(1) == 0)
    def _():
        u_sc[0:HALO, :] = jnp.zeros((HALO, D_MODEL), F32)

    @pl.when(pl.program_id(1) != 0)
    def _():
        u_sc[0:HALO, :] = u_sc[tm:tm + HALO, :]

    u_sc[HALO:HALO + tm, :] = u

    first_tap = HALO - (CONV_WIDTH - 1)
    taps = [[] for _ in range(SUBLANES)]
    for j in range(CONV_WIDTH):
        taps[(first_tap + j) % SUBLANES].append(((first_tap + j) // SUBLANES, j))
    w_dw = w_dw_ref[...]
    b_dw = b_dw_ref[...]

    for c0 in range(0, D_MODEL, CONV_COLS):
        cols = slice(c0, c0 + CONV_COLS)

        def partial_sum(r, r0, nrows):
            return sum(u_sc[r0 + SUBLANES * a:r0 + SUBLANES * a + nrows, cols] * w_dw[j:j + 1, cols]
                       for a, j in taps[r])

        for r0 in range(0, tm + SUBLANES, CONV_ROWS):
            nrows = min(CONV_ROWS, tm + SUBLANES - r0)
            for r in range(1, SUBLANES):
                v_sc[r - 1, r0:r0 + nrows, :] = partial_sum(r, r0, nrows)
        for r0 in range(0, tm, CONV_ROWS):
            acc = b_dw[:, cols] + partial_sum(0, r0, CONV_ROWS)
            for r in range(1, SUBLANES):
                acc = acc + v_sc[r - 1, r0 + r:r0 + r + CONV_ROWS, :]
            c_sc[r0:r0 + CONV_ROWS, cols] = acc

    cv = c_sc[...]
    mu = jnp.mean(cv, axis=-1, keepdims=True)
    xc = cv - mu
    ln = xc * lax.rsqrt(jnp.mean(xc * xc, axis=-1, keepdims=True) + NORM_EPS)
    ln = ln * ln_g_ref[...] + ln_b_ref[...]
    act = (ln * jax.nn.sigmoid(ln)).astype(BF16)
    y = _dot(act, w_pw2_ref[...]) + b_pw2_ref[...]
    x = x + mod[2:3] * _rms(y, ng[1:2])
    out_ref[...] = _ffn_sublayer(x, mod, ng, w1_ref, w2_ref)


def _conv_layer(x, mod, ng, w, w1, w2, layer):
    batch, seq, _ = x.shape
    tm = TOKEN_TILE
    tile = pl.BlockSpec((None, tm, D_MODEL), lambda b, t: (b, t, 0))
    names = ("w_pw1", "b_pw1", "w_dw", "b_dw", "ln_g", "ln_b", "w_pw2", "b_pw2")
    return pl.pallas_call(
        _conv_kernel,
        grid=(batch, seq // tm),
        in_specs=[tile,
                  pl.BlockSpec((None, 6, D_MODEL), lambda b, t: (b, 0, 0)),
                  _const_spec((4, D_MODEL))]
                 + [_const_spec(w[n].shape) for n in names]
                 + [_layer_spec(w1, layer), _layer_spec(w2, layer)],
        out_specs=tile,
        out_shape=jax.ShapeDtypeStruct(x.shape, F32),
        scratch_shapes=[pltpu.VMEM((HALO + tm, D_MODEL), F32),
                        pltpu.VMEM((SUBLANES - 1, tm + SUBLANES, CONV_COLS), F32),
                        pltpu.VMEM((tm, D_MODEL), F32)],
        compiler_params=_params("arbitrary", "arbitrary"),
        name="conv_ffn",
    )(x, mod, ng, *[w[n] for n in names], w1, w2)


def _pool_kernel(x_ref, mod_ref, ng_ref, w_ref, b_ref, scale_ref, w1_ref, w2_ref, out_ref,
                 h_sc, s_a, s_b, y_sc):
    tm = TOKEN_TILE
    n_groups = len(POOL_WINDOWS)
    width = D_MODEL // n_groups
    mod = mod_ref[...]
    ng = ng_ref[...]
    x = x_ref[...]
    h = _rms(x, ng[0:1]) * (1.0 + mod[1:2]) + mod[0:1]

    @pl.when(pl.program_id(1) == 0)
    def _():
        h_sc[0:HALO, :] = jnp.zeros((HALO, D_MODEL), F32)

    @pl.when(pl.program_id(1) != 0)
    def _():
        h_sc[0:HALO, :] = h_sc[tm:tm + HALO, :]

    h_sc[HALO:HALO + tm, :] = h

    n = HALO + tm
    s_a[8:n, :] = h_sc[8:n, :] + h_sc[7:n - 1, :]
    s_b[16:n, width:] = s_a[16:n, width:] + s_a[14:n - 2, width:]
    s_a[24:n, 2 * width:] = s_b[24:n, 2 * width:] + s_b[20:n - 4, 2 * width:]
    s_b[32:n, 3 * width:] = s_a[32:n, 3 * width:] + s_a[24:n - 8, 3 * width:]

    t_idx = pl.program_id(1) * tm + lax.broadcasted_iota(jnp.int32, (tm, 1), 0)
    sums = (s_a, s_b, s_a, s_b)
    for g, win in enumerate(POOL_WINDOWS):
        cols = slice(g * width, (g + 1) * width)
        cnt = jnp.minimum(t_idx + 1, win).astype(F32)
        p = sums[g][HALO:n, cols] / cnt - h[:, cols]
        y_sc[:, cols] = _dot(p.astype(BF16), w_ref[g]) + b_ref[g:g + 1, :]

    y = y_sc[...] * scale_ref[...]
    x = x + mod[2:3] * _rms(y, ng[1:2])
    out_ref[...] = _ffn_sublayer(x, mod, ng, w1_ref, w2_ref)


def _pool_layer(x, mod, ng, w, b, scale, w1, w2, layer):
    batch, seq, _ = x.shape
    tm = TOKEN_TILE
    tile = pl.BlockSpec((None, tm, D_MODEL), lambda b_, t: (b_, t, 0))
    halo_buf = pltpu.VMEM((HALO + tm, D_MODEL), F32)
    return pl.pallas_call(
        _pool_kernel,
        grid=(batch, seq // tm),
        in_specs=[tile,
                  pl.BlockSpec((None, 6, D_MODEL), lambda b_, t: (b_, 0, 0)),
                  _const_spec((4, D_MODEL)),
                  _const_spec(w.shape), _const_spec(b.shape), _const_spec(scale.shape),
                  _layer_spec(w1, layer), _layer_spec(w2, layer)],
        out_specs=tile,
        out_shape=jax.ShapeDtypeStruct(x.shape, F32),
        scratch_shapes=[halo_buf, halo_buf, halo_buf, pltpu.VMEM((tm, D_MODEL), F32)],
        compiler_params=_params("arbitrary", "arbitrary"),
        name="pool_ffn",
    )(x, mod, ng, w, b, scale, w1, w2)


def _pad_heads(w, per_head, lane0=0):
    k = w.shape[0]
    w = w.reshape(k, MLA_HEADS, per_head)
    w = jnp.pad(w, ((0, 0), (0, 0), (lane0, HEAD_PAD - lane0 - per_head)))
    return w.reshape(k, MLA_HEADS * HEAD_PAD)


def _swap_halves(w):
    half = w.shape[-1] // 2
    return jnp.concatenate([w[..., half:], w[..., :half]], axis=-1)


def _mla_weights(w_dq, q_norm_g, w_uq, w_dkv, kv_norm_g, w_ukv, w_o):
    uq = w_uq.reshape(Q_LORA, MLA_HEADS, QK_NOPE + QK_ROPE)
    uq_nope = uq[..., :QK_NOPE].reshape(Q_LORA, -1)
    uq_rope = uq[..., QK_NOPE:]
    rope_cols = lambda w3: _pad_heads(w3.reshape(Q_LORA, -1), QK_ROPE, ROPE_LANE0)
    w_uq_pad = _pad_heads(uq_nope, QK_NOPE) + rope_cols(uq_rope)

    dkv_rope = w_dkv[:, KV_LORA:]
    pad_rope = lambda w: jnp.pad(w, ((0, 0), (ROPE_LANE0, LANES - ROPE_LANE0 - QK_ROPE)))
    w_dkv_all = jnp.concatenate(
        [w_dkv[:, :KV_LORA], pad_rope(dkv_rope), pad_rope(_swap_halves(dkv_rope))], axis=-1)

    ukv = w_ukv.reshape(KV_LORA, MLA_HEADS, QK_NOPE + V_HEAD)
    w_uk = _pad_heads(ukv[..., :QK_NOPE].reshape(KV_LORA, -1), QK_NOPE)
    w_uv = _pad_heads(ukv[..., QK_NOPE:].reshape(KV_LORA, -1), V_HEAD)

    return {
        "w_dq": w_dq.astype(BF16), "qg": q_norm_g.reshape(1, -1),
        "w_uq": w_uq_pad.astype(BF16),
        "w_dkv": w_dkv_all.astype(BF16), "kvg": kv_norm_g.reshape(1, -1),
        "w_uk": w_uk.astype(BF16), "w_uvt": w_uv.T.astype(BF16),
        "w_o": w_o.astype(BF16),
    }


def _rope_inv_freq_row():
    inv_freq = ROPE_THETA ** (-jnp.arange(0, QK_ROPE, 2, dtype=F32) / QK_ROPE)
    return jnp.tile(inv_freq, LANES // inv_freq.shape[0]).reshape(1, LANES)


def kernel(x, c, positions, ada_w, ada_b, norm_g, mla_w_dq, mla_q_norm_g, mla_w_uq, mla_w_dkv, mla_kv_norm_g, mla_w_ukv, mla_w_o, conv_w_pw1, conv_b_pw1, conv_w_dw, conv_b_dw, conv_ln_g, conv_ln_b, conv_w_pw2, conv_b_pw2, pool_w, pool_b, pool_scale, ffn_w1, ffn_w2):
    batch, seq, d = x.shape
    assert d == D_MODEL and seq % TOKEN_TILE == 0 and seq % ATTN_TILE == 0
    assert ATTN_TILE % CHUNK == 0 and TOKEN_TILE % CONV_ROWS == 0
    assert TOKEN_TILE == ATTN_TILE

    mod = _ada_mod(c, ada_w, ada_b)
    pos3 = positions.reshape(batch, seq, 1)
    invf = _rope_inv_freq_row()
    row = lambda v: v.reshape(1, -1)

    w1 = ffn_w1.astype(BF16)
    w2 = ffn_w2.astype(BF16)
    for i in range(DEPTH):
        kind = i % N_MIXERS
        j = i // N_MIXERS
        if kind == 0:
            w = _mla_weights(mla_w_dq[j], mla_q_norm_g[j], mla_w_uq[j], mla_w_dkv[j],
                             mla_kv_norm_g[j], mla_w_ukv[j], mla_w_o[j])
            q, k, vt = _mla_proj(x, pos3, mod[i], norm_g[i], invf, w)
            o = _attention(q, k, vt)
            x = _mla_out(x, o, mod[i], norm_g[i], w["w_o"], w1, w2, i)
        elif kind == 1:
            w = {
                "w_pw1": conv_w_pw1[j].astype(BF16), "b_pw1": row(conv_b_pw1[j]),
                "w_dw": conv_w_dw[j], "b_dw": row(conv_b_dw[j]),
                "ln_g": row(conv_ln_g[j]), "ln_b": row(conv_ln_b[j]),
                "w_pw2": conv_w_pw2[j].astype(BF16), "b_pw2": row(conv_b_pw2[j]),
            }
            x = _conv_layer(x, mod[i], norm_g[i], w, w1, w2, i)
        else:
            x = _pool_layer(x, mod[i], norm_g[i], pool_w[j].astype(BF16), pool_b[j],
                            row(pool_scale[j]), w1, w2, i)
    return x
```
